```python
import jax, jax.numpy as jnp
from jax import lax
import numpy as np

D_MODEL = 2048
BATCH = 4
SEQ = 8192
DEPTH = 4
DEC_BATCH = 1
DEC_SEQ = 16384
PAST_LEN = 128

CONV_DIM = 512
CONV_WIDTH = 3
NA_HEADS = 12
NA_HEAD_DIM = 64
NA_DIM = NA_HEADS * NA_HEAD_DIM
GRID_W = 64
WIN_R_MAX = 8
WIN_C = 16
Q_COL_BLK = 16
K_COL_BLK = 32
N_COL_BLK = GRID_W // Q_COL_BLK
HG_HEADS = 6
HG_DK = 128
HG_DV = 128
HG_DIM = HG_HEADS * HG_DK
HG_CHUNK = 64
F_MIN = 1e-30
D_FF = ((8 * D_MODEL + 3 * 256 - 1) // (3 * 256)) * 256
EPS = 1e-6
NEG_INF = -1e30
SPLIT_WIDTHS = (CONV_DIM,) * 3 + (NA_DIM,) * 3 + (HG_DIM,) * 5 + (D_MODEL,) * 3
D_IN = sum(SPLIT_WIDTHS)

kernel_name = 'hybrid_conv_natten_hgrn2_encoder'


def _split_points():
    return [int(v) for v in np.cumsum(SPLIT_WIDTHS)[:-1]]


def _rmsnorm(x, g):
    xf = x.astype(jnp.float32)
    y = xf * lax.rsqrt(jnp.mean(xf * xf, axis=-1, keepdims=True) + EPS)
    return (y * g.astype(jnp.float32)).astype(x.dtype)


def _short_conv_mixer(h, gate_b, gate_c, w):
    u = gate_c * h
    up = jnp.pad(u, ((0, 0), (1, 1), (0, 0)))
    y = up[:, :-2] * w[0] + up[:, 1:-1] * w[1] + up[:, 2:] * w[2]
    return gate_b * y


def _col_tables():
    c = np.arange(GRID_W).reshape(N_COL_BLK, Q_COL_BLK)
    k_start = np.clip(np.arange(N_COL_BLK) * Q_COL_BLK - (K_COL_BLK - Q_COL_BLK) // 2, 0, GRID_W - K_COL_BLK)
    col_ids = k_start[:, None] + np.arange(K_COL_BLK)
    cs = np.clip(c - WIN_C // 2, 0, GRID_W - WIN_C)
    kc = col_ids[:, None, :]
    mask = (kc >= cs[:, :, None]) & (kc < cs[:, :, None] + WIN_C)
    dc = np.clip(kc - c[:, :, None] + WIN_C - 1, 0, 2 * WIN_C - 2)
    return col_ids, mask, dc


def _neighbourhood_attention(q, k, v, rpb):
    B, L, _ = q.shape
    rows = L // GRID_W
    wr = min(WIN_R_MAX, rows)
    grid = lambda t: t.reshape(B, rows, GRID_W, NA_HEADS, NA_HEAD_DIM)
    qg, kg, vg = grid(q), grid(k), grid(v)
    col_ids, mask, dc = _col_tables()
    col_ids = jnp.asarray(col_ids)
    mask = jnp.asarray(mask)[:, None, :, None, :]
    dc = jnp.asarray(dc)
    scale = NA_HEAD_DIM ** -0.5

    def one_row(r):
        rs = jnp.clip(r - wr // 2, 0, rows - wr)

        def gather_block(t):
            t = lax.dynamic_slice_in_dim(t, rs, wr, axis=1)[:, :, col_ids]
            return t.transpose(0, 2, 1, 3, 4, 5).reshape(B, N_COL_BLK, wr * K_COL_BLK, NA_HEADS, NA_HEAD_DIM)

        kb, vb = gather_block(kg), gather_block(vg)
        qr = lax.dynamic_index_in_dim(qg, r, axis=1, keepdims=False)
        qr = qr.reshape(B, N_COL_BLK, Q_COL_BLK, NA_HEADS, NA_HEAD_DIM)
        dr = rs + jnp.arange(wr) - r + WIN_R_MAX - 1
        bias = rpb[:, dr][:, :, dc]
        bias = jnp.where(mask, bias.transpose(2, 0, 3, 1, 4).astype(jnp.float32), NEG_INF)
        bias = bias.reshape(N_COL_BLK, NA_HEADS, Q_COL_BLK, wr * K_COL_BLK)
        s = jnp.einsum('bjqhd,bjkhd->bjhqk', qr, kb).astype(jnp.float32) * scale + bias
        p = jax.nn.softmax(s, axis=-1).astype(v.dtype)
        o = jnp.einsum('bjhqk,bjkhd->bjqhd', p, vb)
        return o.reshape(B, GRID_W, NA_DIM)

    out = lax.map(one_row, jnp.arange(rows))
    return out.transpose(1, 0, 2, 3).reshape(B, L, NA_DIM)


def _gla_chunk_scan(q, k, v, g):
    B, L, H, dk = q.shape
    dv = v.shape[-1]
    n = L // HG_CHUNK
    to_chunks = lambda t: t.reshape(B, n, HG_CHUNK, H, t.shape[-1]).transpose(1, 0, 3, 2, 4)
    tri = jnp.tril(jnp.ones((HG_CHUNK, HG_CHUNK), dtype=bool))

    def step(S, inp):
        qc, kc, vc, gc = inp
        b = jnp.cumsum(gc, axis=2)
        inter = jnp.einsum('bhsc,bhcv->bhsv', qc * jnp.exp(b), S)
        diff = b[:, :, :, None, :] - b[:, :, None, :, :]
        decay = jnp.where(tri[:, :, None], jnp.exp(jnp.minimum(diff, 0.0)), 0.0)
        A = jnp.einsum('bhsc,bhuc,bhsuc->bhsu', qc, kc, decay)
        intra = jnp.einsum('bhsu,bhuv->bhsv', A, vc)
        b_last = b[:, :, -1:, :]
        S = jnp.exp(b_last[:, :, 0, :, None]) * S + jnp.einsum('bhuc,bhuv->bhcv', kc * jnp.exp(b_last - b), vc)
        return S, inter + intra

    S0 = jnp.zeros((B, H, dk, dv), jnp.float32)
    _, o = lax.scan(step, S0, (to_chunks(q), to_chunks(k), to_chunks(v), to_chunks(g)))
    return o.transpose(1, 0, 3, 2, 4).reshape(B, L, H, dv)


def _hgrn2_mixer(cq, cf_fwd, cf_bwd, ci, cg, lb, norm_g):
    B, L, _ = cq.shape
    heads = lambda t: t.astype(jnp.float32).reshape(B, L, HG_HEADS, -1)
    q = jax.nn.silu(heads(cq))
    v = heads(ci)

    def run(zf, lb_d, rev):
        lb_h = lb_d.reshape(HG_HEADS, HG_DK)
        f = lb_h + (1.0 - lb_h) * jax.nn.sigmoid(heads(zf))
        logf = jnp.log(jnp.maximum(f, F_MIN))
        k = 1.0 - f
        args = (q, k, v, logf)
        if rev:
            args = tuple(jnp.flip(t, axis=1) for t in args)
        o = _gla_chunk_scan(*args)
        return jnp.flip(o, axis=1) if rev else o

    o = run(cf_fwd, lb[0], False) + run(cf_bwd, lb[1], True)
    o = o * lax.rsqrt(jnp.mean(o * o, axis=-1, keepdims=True) + EPS)
    o = o.reshape(B, L, HG_DIM) * norm_g.astype(jnp.float32)
    return (o * jax.nn.silu(cg.astype(jnp.float32))).astype(cq.dtype)


def _trunk(x, norm1_g, w_in, conv_w, rpb, lb_all, hg_norm_g, w_br_conv, w_br_attn, w_br_hgrn,
           w_mix_out, norm2_g, w_ffn_gate, w_ffn_up, w_ffn_down, final_g):
    for l in range(DEPTH):
        h = _rmsnorm(x, norm1_g[l])
        p = jnp.einsum('bld,de->ble', h, w_in[l])
        (a_h, a_b, a_c, nq, nk, nv, cq, cff, cfb, ci, cg, ga, gb, gc) = jnp.split(p, _split_points(), axis=-1)
        y_a = _short_conv_mixer(a_h, a_b, a_c, conv_w[l])
        y_b = _neighbourhood_attention(nq, nk, nv, rpb[l])
        y_c = _hgrn2_mixer(cq, cff, cfb, ci, cg, lb_all[l], hg_norm_g[l])
        mix = (jax.nn.sigmoid(ga) * (y_a @ w_br_conv[l])
               + jax.nn.sigmoid(gb) * (y_b @ w_br_attn[l])
               + jax.nn.sigmoid(gc) * (y_c @ w_br_hgrn[l]))
        x = x + mix @ w_mix_out[l]
        h = _rmsnorm(x, norm2_g[l])
        x = x + (jax.nn.silu(h @ w_ffn_gate[l]) * (h @ w_ffn_up[l])) @ w_ffn_down[l]
    return _rmsnorm(x, final_g)


def setup_inputs(seed: int = 0) -> dict:
    key = jax.random.key(seed)
    ks = jax.random.split(key, 20)
    nrm = lambda k, shape, s: jax.random.normal(k, shape, jnp.float32) * s
    return {
        'x_prompt': nrm(ks[0], (BATCH, SEQ, D_MODEL), 1.0),
        'x_sample': nrm(ks[1], (DEC_BATCH, DEC_SEQ, D_MODEL), 1.0),
        'norm1_g': 1.0 + nrm(ks[2], (DEPTH, D_MODEL), 0.02),
        'w_in': nrm(ks[3], (DEPTH, D_MODEL, D_IN), D_MODEL ** -0.5),
        'conv_w': nrm(ks[4], (DEPTH, CONV_WIDTH, CONV_DIM), CONV_WIDTH ** -0.5),
        'rpb': nrm(ks[5], (DEPTH, NA_HEADS, 2 * WIN_R_MAX - 1, 2 * WIN_C - 1), 0.5),
        'hg_lower': nrm(ks[6], (DEPTH, 2, HG_DIM), 0.5),
        'hg_norm_g': 1.0 + nrm(ks[7], (DEPTH, HG_DIM), 0.02),
        'w_br_conv': nrm(ks[8], (DEPTH, CONV_DIM, D_MODEL), CONV_DIM ** -0.5),
        'w_br_attn': nrm(ks[9], (DEPTH, NA_DIM, D_MODEL), NA_DIM ** -0.5),
        'w_br_hgrn': nrm(ks[10], (DEPTH, HG_DIM, D_MODEL), HG_DIM ** -0.5),
        'w_mix_out': nrm(ks[11], (DEPTH, D_MODEL, D_MODEL), D_MODEL ** -0.5),
        'norm2_g': 1.0 + nrm(ks[12], (DEPTH, D_MODEL), 0.02),
        'w_ffn_gate': nrm(ks[13], (DEPTH, D_MODEL, D_FF), D_MODEL ** -0.5),
        'w_ffn_up': nrm(ks[14], (DEPTH, D_MODEL, D_FF), D_MODEL ** -0.5),
        'w_ffn_down': nrm(ks[15], (DEPTH, D_FF, D_MODEL), D_FF ** -0.5),
        'final_g': 1.0 + nrm(ks[16], (D_MODEL,), 0.02),
    }


def reference(x_prompt, x_sample, norm1_g, w_in, conv_w, rpb, hg_lower, hg_norm_g, w_br_conv, w_br_attn,
              w_br_hgrn, w_mix_out, norm2_g, w_ffn_gate, w_ffn_up, w_ffn_down, final_g):
    sm = jax.nn.softmax(hg_lower.astype(jnp.float32), axis=0)
    lb_all = jnp.cumsum(sm, axis=0) - sm[0]
    y_prompt = _trunk(x_prompt, norm1_g, w_in, conv_w, rpb, lb_all, hg_norm_g, w_br_conv, w_br_attn,
                      w_br_hgrn, w_mix_out, norm2_g, w_ffn_gate, w_ffn_up, w_ffn_down, final_g)
    y_sample = _trunk(x_sample, norm1_g, w_in, conv_w, rpb, lb_all, hg_norm_g, w_br_conv, w_br_attn,
                      w_br_hgrn, w_mix_out, norm2_g, w_ffn_gate, w_ffn_up, w_ffn_down, final_g)
    return (y_prompt, y_sample)
```

```python
import functools

import numpy as np
import jax
import jax.numpy as jnp
from jax import lax
from jax.experimental import pallas as pl
from jax.experimental.pallas import tpu as pltpu

F32 = jnp.float32
BF16 = jnp.bfloat16

D_MODEL = 2048
CONV_DIM = 512
NA_HEADS = 12
NA_HEAD_DIM = 64
NA_DIM = NA_HEADS * NA_HEAD_DIM
GRID_W = 64
WIN_R = 8
WIN_C = 16
HG_HEADS = 6
HG_DK = 128
HG_DIM = HG_HEADS * HG_DK
F_MIN = 1e-30
D_FF = 5632
EPS = 1e-6
NEG_INF = -1e30

P16_WIDTH = 12288
P32_WIDTH = 2 * HG_DIM
COL768 = dict(nq=2, nk=3, nv=4, cq=5, ci=6, cg=7)
COL512 = dict(a_h=0, a_b=1, a_c=2)
COL2048 = dict(ga=3, gb=4, gc=5)

VMEM_LIMIT_BYTES = 56 * 1024 * 1024

IN_TM, IN_TN = 1024, 768
FFN_TM, FFN_TF = 512, 512
MIX_TM = 256
NORM_TM = 512
NA_ROWS = 8
NA_SUB = 4
HG_T = 128
CONV_HALO = 16


def _cparams(sem):
    return pltpu.CompilerParams(dimension_semantics=sem, vmem_limit_bytes=VMEM_LIMIT_BYTES)


def _rms_scale(x, g):
    ms = jnp.mean(x * x, axis=-1, keepdims=True)
    return x * lax.rsqrt(ms + EPS) * g


def _sigmoid(x):
    return 1.0 / (1.0 + jnp.exp(-x))


def _inproj_kernel(x_ref, g_ref, w_ref, o16_ref, o32_ref, h_ref, *, n16):
    j = pl.program_id(1)

    @pl.when(j == 0)
    def _():
        h_ref[...] = _rms_scale(x_ref[...], g_ref[...]).astype(BF16)

    acc = jnp.dot(h_ref[...], w_ref[...], preferred_element_type=F32)

    @pl.when(j < n16)
    def _():
        o16_ref[...] = acc.astype(BF16)

    @pl.when(j >= n16)
    def _():
        o32_ref[...] = acc


def _inproj(x, g, w, layer):
    m = x.shape[0]
    n16 = P16_WIDTH // IN_TN
    n32 = P32_WIDTH // IN_TN
    return pl.pallas_call(
        functools.partial(_inproj_kernel, n16=n16),
        grid=(m // IN_TM, n16 + n32),
        in_specs=[
            pl.BlockSpec((IN_TM, D_MODEL), lambda i, j: (i, 0)),
            pl.BlockSpec((None, 1, D_MODEL), lambda i, j: (layer, 0, 0)),
            pl.BlockSpec((None, D_MODEL, IN_TN), lambda i, j: (layer, 0, j)),
        ],
        out_specs=[
            pl.BlockSpec((IN_TM, IN_TN), lambda i, j: (i, jnp.minimum(j, n16 - 1))),
            pl.BlockSpec((IN_TM, IN_TN), lambda i, j: (i, jnp.maximum(j - n16, 0))),
        ],
        out_shape=[jax.ShapeDtypeStruct((m, P16_WIDTH), BF16),
                   jax.ShapeDtypeStruct((m, P32_WIDTH), F32)],
        scratch_shapes=[pltpu.VMEM((IN_TM, D_MODEL), BF16)],
        compiler_params=_cparams(("parallel", "arbitrary")),
        name="inproj",
    )(x, g, w)


def _na_kernel(s0_ref, s1_ref, q_ref, kp_ref, kc_ref, kn_ref, vp_ref, vc_ref, vn_ref, t_ref, o_ref):
    i = pl.program_id(0)
    s0 = s0_ref[i]
    s1 = s1_ref[i]
    nq = NA_SUB * GRID_W
    nk = (NA_SUB + WIN_R) * GRID_W
    lane = lax.broadcasted_iota(jnp.int32, (nq, 2 * NA_HEAD_DIM), 1)
    lo_half = lane < NA_HEAD_DIM
    for sub in range(NA_ROWS // NA_SUB):
        qbase = NA_ROWS * i + NA_SUB * sub
        kbase = NA_ROWS * i - NA_SUB if sub == 0 else NA_ROWS * i
        qrow = qbase + lax.broadcasted_iota(jnp.int32, (nq, nk), 0) // GRID_W
        krow = kbase + lax.broadcasted_iota(jnp.int32, (nq, nk), 1) // GRID_W
        rs = jnp.clip(qrow - WIN_R // 2, s0, s1 - WIN_R)
        off = (krow - rs).astype(jnp.uint32)
        rowmask = jnp.where(off < WIN_R, 0.0, NEG_INF).astype(F32)
        qs = slice(sub * nq, (sub + 1) * nq)
        for hp in range(NA_HEADS // 2):
            cs = slice(hp * 2 * NA_HEAD_DIM, (hp + 1) * 2 * NA_HEAD_DIM)
            if sub == 0:
                k2 = jnp.concatenate([kp_ref[:, cs], kc_ref[:, cs]], axis=0)
                v2 = jnp.concatenate([vp_ref[:, cs], vc_ref[:, cs]], axis=0)
            else:
                k2 = jnp.concatenate([kc_ref[:, cs], kn_ref[:, cs]], axis=0)
                v2 = jnp.concatenate([vc_ref[:, cs], vn_ref[:, cs]], axis=0)
            q2 = q_ref[qs, cs] * (NA_HEAD_DIM ** -0.5)
            outs = []
            for half in range(2):
                keep = lo_half if half == 0 else jnp.logical_not(lo_half)
                qh = jnp.where(keep, q2, jnp.zeros_like(q2))
                s = lax.dot_general(qh, k2, (((1,), (1,)), ((), ())), preferred_element_type=F32)
                s = s + t_ref[2 * hp + half] + rowmask
                mx = jnp.max(s, axis=-1, keepdims=True)
                e = jnp.exp(s - mx)
                den = jnp.sum(e, axis=-1, keepdims=True)
                o = jnp.dot(e.astype(BF16), v2, preferred_element_type=F32)
                outs.append(o * (1.0 / den))
            o_ref[qs, cs] = jnp.where(lo_half, outs[0], outs[1]).astype(BF16)


def _na(p16, t_rel, s0, s1, layer):
    m = p16.shape[0]
    blk = NA_ROWS * GRID_W
    sub = NA_SUB * GRID_W
    nblk = m // blk
    nsub = m // sub
    r = NA_ROWS // NA_SUB

    def cur(col):
        return pl.BlockSpec((blk, NA_DIM), lambda i, a, b: (i, col))

    def prev(col):
        return pl.BlockSpec((sub, NA_DIM), lambda i, a, b: (jnp.maximum(r * i - 1, 0), col))

    def nxt(col):
        return pl.BlockSpec((sub, NA_DIM), lambda i, a, b: (jnp.minimum(r * i + r, nsub - 1), col))

    grid_spec = pltpu.PrefetchScalarGridSpec(
        num_scalar_prefetch=2,
        grid=(nblk,),
        in_specs=[
            cur(COL768["nq"]),
            prev(COL768["nk"]), cur(COL768["nk"]), nxt(COL768["nk"]),
            prev(COL768["nv"]), cur(COL768["nv"]), nxt(COL768["nv"]),
            pl.BlockSpec((None, NA_HEADS, sub, (NA_SUB + WIN_R) * GRID_W),
                         lambda i, a, b: (layer, 0, 0, 0)),
        ],
        out_specs=pl.BlockSpec((blk, NA_DIM), lambda i, a, b: (i, 0)),
    )
    return pl.pallas_call(
        _na_kernel,
        grid_spec=grid_spec,
        out_shape=jax.ShapeDtypeStruct((m, NA_DIM), BF16),
        compiler_params=_cparams(("parallel",)),
        name="natten",
    )(s0, s1, p16, p16, p16, p16, p16, p16, p16, t_rel)


def _na_bias_table(rpb):
    c = np.arange(GRID_W)
    cs = np.clip(c - WIN_C // 2, 0, GRID_W - WIN_C)
    kc = np.arange(GRID_W)[None, :]
    col_ok = (kc >= cs[:, None]) & (kc < cs[:, None] + WIN_C)
    dc = np.clip(kc - c[:, None] + WIN_C - 1, 0, 2 * WIN_C - 2)
    small = jnp.where(jnp.asarray(col_ok), rpb[..., jnp.asarray(dc)].astype(F32), NEG_INF)
    a = np.arange(NA_SUB)[:, None]
    t = np.arange(NA_SUB + WIN_R)[None, :]
    dr = t - NA_SUB - a + WIN_R - 1
    assert dr.min() >= 0 and dr.max() <= 2 * WIN_R - 2
    big = small[:, :, jnp.asarray(dr)]
    big = big.transpose(0, 1, 2, 4, 3, 5)
    d, h = rpb.shape[0], rpb.shape[1]
    return big.reshape(d, h, NA_SUB * GRID_W, (NA_SUB + WIN_R) * GRID_W)


def _hg_unit(q_ref, z_ref, v_ref, lb, o_ref, s_ref, sidx, hs, code, rowi, fwd):
    t = HG_T
    z = z_ref[:, hs]
    f = lb + (1.0 - lb) * _sigmoid(z)
    fc = jnp.maximum(f, F_MIN)
    k = 1.0 - f
    qv = q_ref[:, hs].astype(F32)
    q = qv * _sigmoid(qv)
    v = v_ref[:, hs]
    ones = jnp.ones_like(fc)
    pfx, sfx = (fc, ones) if fwd else (ones, fc)
    tot = fc
    nt = (((1,), (1,)), ((), ()))

    def pair(qm, km):
        return lax.dot_general((q * qm).astype(BF16), (k * km).astype(BF16), nt,
                               preferred_element_type=F32)

    a = jnp.where(code == -1, pair(ones, ones), 0.0)
    half = 1
    lvl = 0
    while half < t:
        qm, km = (pfx, sfx) if fwd else (sfx, pfx)
        a = jnp.where(code == lvl, pair(qm, km), a)
        upper = (rowi & half) != 0
        sib = jnp.where(upper, pltpu.roll(tot, half, 0), pltpu.roll(tot, t - half, 0))
        pfx = pfx * jnp.where(upper, sib, 1.0)
        sfx = sfx * jnp.where(upper, 1.0, sib)
        tot = tot * sib
        half *= 2
        lvl += 1
    qm, km = (pfx, sfx) if fwd else (sfx, pfx)
    st = s_ref[sidx]
    inter = lax.dot_general((q * qm).astype(BF16), st.astype(BF16), nt, preferred_element_type=F32)
    intra = jnp.dot(a.astype(BF16), v, preferred_element_type=F32)
    o_ref[:, hs] = inter + intra
    upd = lax.dot_general(v, (k * km).astype(BF16), (((0,), (0,)), ((), ())),
                          preferred_element_type=F32)
    s_ref[sidx] = st * tot[0:1, :] + upd


def _hg_kernel(rf_ref, rb_ref, qf_ref, zf_ref, vf_ref, qb_ref, zb_ref, vb_ref, lb_ref,
               of_ref, ob_ref, s_ref):
    c = pl.program_id(0)

    @pl.when(rf_ref[c] == 1)
    def _():
        s_ref[0:HG_HEADS] = jnp.zeros((HG_HEADS, HG_DK, HG_DK), F32)

    @pl.when(rb_ref[c] == 1)
    def _():
        s_ref[HG_HEADS:2 * HG_HEADS] = jnp.zeros((HG_HEADS, HG_DK, HG_DK), F32)

    t = HG_T
    row = lax.broadcasted_iota(jnp.int32, (t, t), 0)
    col = lax.broadcasted_iota(jnp.int32, (t, t), 1)
    x = row ^ col
    hb = (pltpu.bitcast(x.astype(F32), jnp.int32) >> 23) - 127
    diag = jnp.where(row == col, -1, -2)
    code_f = jnp.where(row > col, hb, diag)
    code_b = jnp.where(row < col, hb, diag)
    rowi = lax.broadcasted_iota(jnp.int32, (t, HG_DK), 0)
    for h in range(HG_HEADS):
        hs = slice(h * HG_DK, (h + 1) * HG_DK)
        _hg_unit(qf_ref, zf_ref, vf_ref, lb_ref[0:1, hs], of_ref, s_ref, h, hs, code_f, rowi, True)
        _hg_unit(qb_ref, zb_ref, vb_ref, lb_ref[1:2, hs], ob_ref, s_ref, HG_HEADS + h, hs,
                 code_b, rowi, False)


def _hgrn(p16, p32, lb, reset_f, reset_b, layer):
    m = p16.shape[0]
    n = m // HG_T

    def fspec(col):
        return pl.BlockSpec((HG_T, HG_DIM), lambda c, a, b: (c, col))

    def bspec(col):
        return pl.BlockSpec((HG_T, HG_DIM), lambda c, a, b: (n - 1 - c, col))

    grid_spec = pltpu.PrefetchScalarGridSpec(
        num_scalar_prefetch=2,
        grid=(n,),
        in_specs=[
            fspec(COL768["cq"]), fspec(0), fspec(COL768["ci"]),
            bspec(COL768["cq"]), bspec(1), bspec(COL768["ci"]),
            pl.BlockSpec((None, 2, HG_DIM), lambda c, a, b: (layer, 0, 0)),
        ],
        out_specs=[
            pl.BlockSpec((HG_T, HG_DIM), lambda c, a, b: (c, 0)),
            pl.BlockSpec((HG_T, HG_DIM), lambda c, a, b: (n - 1 - c, 0)),
        ],
        scratch_shapes=[pltpu.VMEM((2 * HG_HEADS, HG_DK, HG_DK), F32)],
    )
    return pl.pallas_call(
        _hg_kernel,
        grid_spec=grid_spec,
        out_shape=[jax.ShapeDtypeStruct((m, HG_DIM), F32), jax.ShapeDtypeStruct((m, HG_DIM), F32)],
        compiler_params=_cparams(("arbitrary",)),
        name="hgrn2",
    )(reset_f, reset_b, p16, p32, p16, p16, p32, p16, lb)


def _mix_kernel(st_ref, en_ref, x_ref, ah_ref, ab_ref, ac_ref, ahp_ref, acp_ref, ahn_ref, acn_ref,
                cw_ref, yb_ref, of_ref, ob_ref, cg_ref, ng_ref, ga_ref, gb_ref, gc_ref,
                wa_ref, wb_ref, wc_ref, wo_ref, o_ref):
    i = pl.program_id(0)
    tm = x_ref.shape[0]
    u = ac_ref[...].astype(F32) * ah_ref[...].astype(F32)
    keep_p = jnp.where(st_ref[i] == 1, 0.0, 1.0)
    keep_n = jnp.where(en_ref[i] == 1, 0.0, 1.0)
    h = CONV_HALO
    up_edge = acp_ref[h - 1:h, :].astype(F32) * ahp_ref[h - 1:h, :].astype(F32) * keep_p
    un_edge = acn_ref[0:1, :].astype(F32) * ahn_ref[0:1, :].astype(F32) * keep_n
    rowi = lax.broadcasted_iota(jnp.int32, u.shape, 0)
    u_prev = jnp.where(rowi == 0, up_edge, pltpu.roll(u, 1, 0))
    u_next = jnp.where(rowi == tm - 1, un_edge, pltpu.roll(u, tm - 1, 0))
    cw = cw_ref[...]
    y_a = ab_ref[...].astype(F32) * (u_prev * cw[0:1] + u * cw[1:2] + u_next * cw[2:3])
    cg = cg_ref[...].astype(F32)
    gate = cg * _sigmoid(cg)
    ng = ng_ref[...]
    parts = []
    for hd in range(HG_HEADS):
        hs = slice(hd * HG_DK, (hd + 1) * HG_DK)
        o = of_ref[:, hs] + ob_ref[:, hs]
        o = o * lax.rsqrt(jnp.mean(o * o, axis=-1, keepdims=True) + EPS)
        parts.append(o * ng[:, hs] * gate[:, hs])
    y_c = jnp.concatenate(parts, axis=-1)
    mix = _sigmoid(ga_ref[...].astype(F32)) * jnp.dot(y_a.astype(BF16), wa_ref[...],
                                                      preferred_element_type=F32)
    mix = mix + _sigmoid(gb_ref[...].astype(F32)) * jnp.dot(yb_ref[...], wb_ref[...],
                                                            preferred_element_type=F32)
    mix = mix + _sigmoid(gc_ref[...].astype(F32)) * jnp.dot(y_c.astype(BF16), wc_ref[...],
                                                            preferred_element_type=F32)
    o_ref[...] = x_ref[...] + jnp.dot(mix.astype(BF16), wo_ref[...], preferred_element_type=F32)


def _mix(x, p16, y_b, o_f, o_b, conv_w, ng, wa, wb, wc, wo, is_start, is_end, layer):
    m = x.shape[0]
    tm = MIX_TM
    hb = tm // CONV_HALO
    nh = m // CONV_HALO

    def tok(width, col):
        return pl.BlockSpec((tm, width), lambda i, a, b: (i, col))

    def halo_prev(col):
        return pl.BlockSpec((CONV_HALO, CONV_DIM), lambda i, a, b: (jnp.maximum(i * hb - 1, 0), col))

    def halo_next(col):
        return pl.BlockSpec((CONV_HALO, CONV_DIM),
                            lambda i, a, b: (jnp.minimum((i + 1) * hb, nh - 1), col))

    def weight(rows):
        return pl.BlockSpec((None, rows, D_MODEL), lambda i, a, b: (layer, 0, 0),
                            pipeline_mode=pl.Buffered(1))

    grid_spec = pltpu.PrefetchScalarGridSpec(
        num_scalar_prefetch=2,
        grid=(m // tm,),
        in_specs=[
            tok(D_MODEL, 0),
            tok(CONV_DIM, COL512["a_h"]), tok(CONV_DIM, COL512["a_b"]), tok(CONV_DIM, COL512["a_c"]),
            halo_prev(COL512["a_h"]), halo_prev(COL512["a_c"]),
            halo_next(COL512["a_h"]), halo_next(COL512["a_c"]),
            pl.BlockSpec((None, 3, CONV_DIM), lambda i, a, b: (layer, 0, 0)),
            tok(NA_DIM, 0), tok(HG_DIM, 0), tok(HG_DIM, 0), tok(HG_DIM, COL768["cg"]),
            pl.BlockSpec((None, 1, HG_DIM), lambda i, a, b: (layer, 0, 0)),
            tok(D_MODEL, COL2048["ga"]), tok(D_MODEL, COL2048["gb"]), tok(D_MODEL, COL2048["gc"]),
            weight(CONV_DIM), weight(NA_DIM), weight(HG_DIM), weight(D_MODEL),
        ],
        out_specs=pl.BlockSpec((tm, D_MODEL), lambda i, a, b: (i, 0)),
    )
    return pl.pallas_call(
        _mix_kernel,
        grid_spec=grid_spec,
        out_shape=jax.ShapeDtypeStruct((m, D_MODEL), F32),
        compiler_params=_cparams(("parallel",)),
        name="mix",
    )(is_start, is_end, x, p16, p16, p16, p16, p16, p16, p16, conv_w, y_b, o_f, o_b, p16, ng,
      p16, p16, p16, wa, wb, wc, wo)


def _ffn_kernel(x_ref, g_ref, wg_ref, wu_ref, wd_ref, o_ref, h_ref):
    j = pl.program_id(1)

    @pl.when(j == 0)
    def _():
        x = x_ref[...]
        h_ref[...] = _rms_scale(x, g_ref[...]).astype(BF16)
        o_ref[...] = x

    h = h_ref[...]
    g = jnp.dot(h, wg_ref[...], preferred_element_type=F32)
    u = jnp.dot(h, wu_ref[...], preferred_element_type=F32)
    a = (g * _sigmoid(g) * u).astype(BF16)
    o_ref[...] += jnp.dot(a, wd_ref[...], preferred_element_type=F32)


def _ffn(x, g, wg, wu, wd, layer):
    m = x.shape[0]
    return pl.pallas_call(
        _ffn_kernel,
        grid=(m // FFN_TM, D_FF // FFN_TF),
        in_specs=[
            pl.BlockSpec((FFN_TM, D_MODEL), lambda i, j: (i, 0)),
            pl.BlockSpec((None, 1, D_MODEL), lambda i, j: (layer, 0, 0)),
            pl.BlockSpec((None, D_MODEL, FFN_TF), lambda i, j: (layer, 0, j)),
            pl.BlockSpec((None, D_MODEL, FFN_TF), lambda i, j: (layer, 0, j)),
            pl.BlockSpec((None, FFN_TF, D_MODEL), lambda i, j: (layer, j, 0)),
        ],
        out_specs=pl.BlockSpec((FFN_TM, D_MODEL), lambda i, j: (i, 0)),
        out_shape=jax.ShapeDtypeStruct((m, D_MODEL), F32),
        scratch_shapes=[pltpu.VMEM((FFN_TM, D_MODEL), BF16)],
        compiler_params=_cparams(("parallel", "arbitrary")),
        name="ffn",
    )(x, g, wg, wu, wd)


def _norm_kernel(x_ref, g_ref, o_ref):
    o_ref[...] = _rms_scale(x_ref[...], g_ref[...])


def _final_norm(x, g, tok_start, tok_len):
    off = tok_start // NORM_TM
    return pl.pallas_call(
        _norm_kernel,
        grid=(tok_len // NORM_TM,),
        in_specs=[
            pl.BlockSpec((NORM_TM, D_MODEL), lambda i: (i + off, 0)),
            pl.BlockSpec((1, D_MODEL), lambda i: (0, 0)),
        ],
        out_specs=pl.BlockSpec((NORM_TM, D_MODEL), lambda i: (i, 0)),
        out_shape=jax.ShapeDtypeStruct((tok_len, D_MODEL), F32),
        compiler_params=_cparams(("parallel",)),
        name="final_norm",
    )(x, g)


def _descriptors(seqs, m):
    na_blk = NA_ROWS * GRID_W
    s0 = np.zeros(m // na_blk, np.int32)
    s1 = np.zeros(m // na_blk, np.int32)
    reset_f = np.zeros(m // HG_T, np.int32)
    reset_b = np.zeros(m // HG_T, np.int32)
    is_start = np.zeros(m // MIX_TM, np.int32)
    is_end = np.zeros(m // MIX_TM, np.int32)
    n_chunks = m // HG_T
    for start, length in seqs:
        assert start % na_blk == 0 and length % na_blk == 0 and length // GRID_W >= 2 * WIN_R
        assert start % MIX_TM == 0 and length % MIX_TM == 0
        end = start + length
        s0[start // na_blk:end // na_blk] = start // GRID_W
        s1[start // na_blk:end // na_blk] = end // GRID_W
        reset_f[start // HG_T] = 1
        reset_b[n_chunks - 1 - (end // HG_T - 1)] = 1
        is_start[start // MIX_TM] = 1
        is_end[end // MIX_TM - 1] = 1
    return tuple(jnp.asarray(a) for a in (s0, s1, reset_f, reset_b, is_start, is_end))


def _trunk(x, seqs, norm1_g, w_in, conv_w, t_rel, lb_all, hg_norm_g, w_br_conv, w_br_attn, w_br_hgrn,
           w_mix_out, norm2_g, w_ffn_gate, w_ffn_up, w_ffn_down):
    m = x.shape[0]
    depth = w_in.shape[0]
    s0, s1, reset_f, reset_b, is_start, is_end = _descriptors(seqs, m)
    for l in range(depth):
        p16, p32 = _inproj(x, norm1_g, w_in, l)
        y_b = _na(p16, t_rel, s0, s1, l)
        o_f, o_b = _hgrn(p16, p32, lb_all, reset_f, reset_b, l)
        x = _mix(x, p16, y_b, o_f, o_b, conv_w, hg_norm_g, w_br_conv, w_br_attn, w_br_hgrn,
                 w_mix_out, is_start, is_end, l)
        x = _ffn(x, norm2_g, w_ffn_gate, w_ffn_up, w_ffn_down, l)
    return x


def _prepare_params(norm1_g, w_in, conv_w, rpb, hg_lower, hg_norm_g, w_br_conv, w_br_attn, w_br_hgrn,
                    w_mix_out, norm2_g, w_ffn_gate, w_ffn_up, w_ffn_down):
    hg0 = 3 * CONV_DIM + 3 * NA_DIM + HG_DIM
    hg1 = hg0 + 2 * HG_DIM
    w_in_r = jnp.concatenate([w_in[:, :, :hg0], w_in[:, :, hg1:], w_in[:, :, hg0:hg1]], axis=-1)
    sm = jax.nn.softmax(hg_lower.astype(F32), axis=0)
    lb_all = jnp.cumsum(sm, axis=0) - sm[0]
    bf = lambda w: w.astype(BF16)
    return (norm1_g[:, None, :].astype(F32), bf(w_in_r), conv_w.astype(F32), _na_bias_table(rpb), lb_all,
            hg_norm_g[:, None, :].astype(F32), bf(w_br_conv), bf(w_br_attn), bf(w_br_hgrn),
            bf(w_mix_out), norm2_g[:, None, :].astype(F32), bf(w_ffn_gate), bf(w_ffn_up),
            bf(w_ffn_down))


def kernel(x_prompt, x_sample, norm1_g, w_in, conv_w, rpb, hg_lower, hg_norm_g, w_br_conv, w_br_attn,
           w_br_hgrn, w_mix_out, norm2_g, w_ffn_gate, w_ffn_up, w_ffn_down, final_g):
    groups = [x_prompt, x_sample]
    seqs = []
    tok = 0
    for g in groups:
        b, length, _ = g.shape
        for _ in range(b):
            seqs.append((tok, length))
            tok += length
    x = jnp.concatenate([g.reshape(-1, D_MODEL) for g in groups], axis=0)
    params = _prepare_params(norm1_g, w_in, conv_w, rpb, hg_lower, hg_norm_g, w_br_conv, w_br_attn,
                             w_br_hgrn, w_mix_out, norm2_g, w_ffn_gate, w_ffn_up, w_ffn_down)
    x = _trunk(x, seqs, *params)
    outs = []
    tok = 0
    fg = final_g[None, :].astype(F32)
    for g in groups:
        n = g.shape[0] * g.shape[1]
        outs.append(_final_norm(x, fg, tok, n).reshape(g.shape))
        tok += n
    return tuple(outs)
```

```python
import functools

import numpy as np
import jax
import jax.numpy as jnp
from jax import lax
from jax.experimental import pallas as pl
from jax.experimental.pallas import tpu as pltpu

F32 = jnp.float32
BF16 = jnp.bfloat16

D_MODEL = 2048
CONV_DIM = 512
NA_HEADS = 12
NA_HEAD_DIM = 64
NA_DIM = NA_HEADS * NA_HEAD_DIM
GRID_W = 64
WIN_R = 8
WIN_C = 16
HG_HEADS = 6
HG_DK = 128
HG_DIM = HG_HEADS * HG_DK
F_MIN = 1e-30
D_FF = 5632
EPS = 1e-6
NEG_INF = -1e30

P32_WIDTH = 2 * HG_DIM
P16_WIDTH = 3 * CONV_DIM + 3 * NA_DIM + 3 * HG_DIM + 3 * D_MODEL
FGATE_COL0 = 3 * CONV_DIM + 3 * NA_DIM + HG_DIM
COL768 = dict(nq=2, nk=3, nv=4, cq=5, ci=6, cg=7)
COL512 = dict(a_h=0, a_b=1, a_c=2)
COL2048 = dict(ga=3, gb=4, gc=5)

VMEM_LIMIT_BYTES = 56 * 1024 * 1024

IN_TM, IN_TN = 1024, 1536
FFN_TM_CHOICES, FFN_TF = (1024, 512), 512
MIX_TM = 256
NA_ROWS = 8
NA_SUB = 4
NA_TABLE_W = (2 * WIN_R - 1) * GRID_W
HG_T = 128
CONV_HALO = 16


def _cparams(sem):
    return pltpu.CompilerParams(dimension_semantics=sem, vmem_limit_bytes=VMEM_LIMIT_BYTES)


def _rms_scale(x, g):
    ms = jnp.mean(x * x, axis=-1, keepdims=True)
    return x * lax.rsqrt(ms + EPS) * g


def _sigmoid(x):
    return 1.0 / (1.0 + jnp.exp(-x))


def _inproj_kernel(x_ref, g_ref, w_ref, o_ref, h_ref):
    @pl.when(pl.program_id(1) == 0)
    def _():
        h_ref[...] = _rms_scale(x_ref[...], g_ref[...]).astype(BF16)

    o_ref[...] = jnp.dot(h_ref[...], w_ref[...], preferred_element_type=F32).astype(BF16)


def _inproj(x, g, w, layer):
    m = x.shape[0]
    nj = P16_WIDTH // IN_TN
    skip = FGATE_COL0 // IN_TN
    nskip = P32_WIDTH // IN_TN
    return pl.pallas_call(
        _inproj_kernel,
        grid=(m // IN_TM, nj),
        in_specs=[
            pl.BlockSpec((IN_TM, D_MODEL), lambda i, j: (i, 0)),
            pl.BlockSpec((None, 1, D_MODEL), lambda i, j: (layer, 0, 0)),
            pl.BlockSpec((None, D_MODEL, IN_TN),
                         lambda i, j: (layer, 0, jnp.where(j >= skip, j + nskip, j))),
        ],
        out_specs=[
            pl.BlockSpec((IN_TM, IN_TN), lambda i, j: (i, j)),
            pl.BlockSpec((IN_TM, D_MODEL), lambda i, j: (i, 0)),
        ],
        out_shape=[jax.ShapeDtypeStruct((m, P16_WIDTH), BF16),
                   jax.ShapeDtypeStruct((m, D_MODEL), BF16)],
        compiler_params=_cparams(("parallel", "arbitrary")),
        name="inproj",
    )(x, g, w)


def _na_bias_rows(t_ref, head):
    nk = (NA_SUB + WIN_R) * GRID_W
    rows = []
    for a in range(NA_SUB):
        shift = NA_SUB - 1 - a
        off = (shift // 2) * 2 * GRID_W
        rows.append(t_ref[head, shift % 2, :, off:off + nk])
    return jnp.concatenate(rows, axis=0)


def _na_kernel(s0_ref, s1_ref, q_ref, kp_ref, kc_ref, kn_ref, vp_ref, vc_ref, vn_ref, t_ref, o_ref):
    i = pl.program_id(0)
    s0 = s0_ref[i]
    s1 = s1_ref[i]
    nq = NA_SUB * GRID_W
    nk = (NA_SUB + WIN_R) * GRID_W
    lane = lax.broadcasted_iota(jnp.int32, (nq, 2 * NA_HEAD_DIM), 1)
    lo_half = lane < NA_HEAD_DIM
    for sub in range(NA_ROWS // NA_SUB):
        qbase = NA_ROWS * i + NA_SUB * sub
        kbase = NA_ROWS * i - NA_SUB if sub == 0 else NA_ROWS * i
        qrow = qbase + lax.broadcasted_iota(jnp.int32, (nq, nk), 0) // GRID_W
        krow = kbase + lax.broadcasted_iota(jnp.int32, (nq, nk), 1) // GRID_W
        rs = jnp.clip(qrow - WIN_R // 2, s0, s1 - WIN_R)
        off = (krow - rs).astype(jnp.uint32)
        rowmask = jnp.where(off < WIN_R, 0.0, NEG_INF).astype(F32)
        qs = slice(sub * nq, (sub + 1) * nq)
        for hp in range(NA_HEADS // 2):
            cs = slice(hp * 2 * NA_HEAD_DIM, (hp + 1) * 2 * NA_HEAD_DIM)
            if sub == 0:
                k2 = jnp.concatenate([kp_ref[:, cs], kc_ref[:, cs]], axis=0)
                v2 = jnp.concatenate([vp_ref[:, cs], vc_ref[:, cs]], axis=0)
            else:
                k2 = jnp.concatenate([kc_ref[:, cs], kn_ref[:, cs]], axis=0)
                v2 = jnp.concatenate([vc_ref[:, cs], vn_ref[:, cs]], axis=0)
            q2 = q_ref[qs, cs] * (NA_HEAD_DIM ** -0.5)
            outs = []
            for half in range(2):
                keep = lo_half if half == 0 else jnp.logical_not(lo_half)
                qh = jnp.where(keep, q2, jnp.zeros_like(q2))
                s = lax.dot_general(qh, k2, (((1,), (1,)), ((), ())), preferred_element_type=F32)
                s = s + _na_bias_rows(t_ref, 2 * hp + half) + rowmask
                mx = jnp.max(s, axis=-1, keepdims=True)
                e = jnp.exp(s - mx)
                den = jnp.sum(e, axis=-1, keepdims=True)
                o = jnp.dot(e.astype(BF16), v2, preferred_element_type=F32)
                outs.append(o * (1.0 / den))
            o_ref[qs, cs] = jnp.where(lo_half, outs[0], outs[1]).astype(BF16)


def _na(p16, t_rel, s0, s1, layer):
    m = p16.shape[0]
    blk = NA_ROWS * GRID_W
    sub = NA_SUB * GRID_W
    nblk = m // blk
    nsub = m // sub
    r = NA_ROWS // NA_SUB

    def cur(col):
        return pl.BlockSpec((blk, NA_DIM), lambda i, a, b: (i, col))

    def prev(col):
        return pl.BlockSpec((sub, NA_DIM), lambda i, a, b: (jnp.maximum(r * i - 1, 0), col))

    def nxt(col):
        return pl.BlockSpec((sub, NA_DIM), lambda i, a, b: (jnp.minimum(r * i + r, nsub - 1), col))

    grid_spec = pltpu.PrefetchScalarGridSpec(
        num_scalar_prefetch=2,
        grid=(nblk,),
        in_specs=[
            cur(COL768["nq"]),
            prev(COL768["nk"]), cur(COL768["nk"]), nxt(COL768["nk"]),
            prev(COL768["nv"]), cur(COL768["nv"]), nxt(COL768["nv"]),
            pl.BlockSpec((None, NA_HEADS, 2, GRID_W, NA_TABLE_W), lambda i, a, b: (layer, 0, 0, 0, 0),
                         pipeline_mode=pl.Buffered(1)),
        ],
        out_specs=pl.BlockSpec((blk, NA_DIM), lambda i, a, b: (i, 0)),
    )
    return pl.pallas_call(
        _na_kernel,
        grid_spec=grid_spec,
        out_shape=jax.ShapeDtypeStruct((m, NA_DIM), BF16),
        compiler_params=_cparams(("parallel",)),
        name="natten",
    )(s0, s1, p16, p16, p16, p16, p16, p16, p16, t_rel)


def _na_bias_table(rpb):
    assert NA_SUB + WIN_R - 1 + NA_SUB - 1 == 2 * WIN_R - 2
    c = np.arange(GRID_W)
    cs = np.clip(c - WIN_C // 2, 0, GRID_W - WIN_C)
    kc = np.arange(GRID_W)[None, :]
    col_ok = (kc >= cs[:, None]) & (kc < cs[:, None] + WIN_C)
    dc = np.clip(kc - c[:, None] + WIN_C - 1, 0, 2 * WIN_C - 2)
    small = jnp.where(jnp.asarray(col_ok), rpb[..., jnp.asarray(dc)].astype(F32), NEG_INF)
    d, h, ndr = rpb.shape[0], rpb.shape[1], rpb.shape[2]
    flat = small.transpose(0, 1, 3, 2, 4).reshape(d, h, GRID_W, ndr * GRID_W)
    shifted = jnp.pad(flat[..., GRID_W:], ((0, 0), (0, 0), (0, 0), (0, GRID_W)))
    return jnp.stack([flat, shifted], axis=2)


def _hg_unit(q_ref, z, v_ref, lb, o_ref, s_ref, sidx, hs, code, rowi, fwd):
    t = HG_T
    f = lb + (1.0 - lb) * _sigmoid(z)
    fc = jnp.maximum(f, F_MIN)
    k = 1.0 - f
    qv = q_ref[:, hs].astype(F32)
    q = qv * _sigmoid(qv)
    v = v_ref[:, hs]
    ones = jnp.ones_like(fc)
    pfx, sfx = (fc, ones) if fwd else (ones, fc)
    tot = fc
    nt = (((1,), (1,)), ((), ()))

    def mm_nt(x, y):
        return lax.dot_general(x.astype(BF16), y.astype(BF16), nt, preferred_element_type=F32)

    a = jnp.where(code == -1, mm_nt(q, k), 0.0)
    half = 1
    lvl = 0
    qside = 1 if fwd else 0
    while half < t:
        qm, km = (pfx, sfx) if fwd else (sfx, pfx)
        if half < 8:
            a = jnp.where(code == lvl, mm_nt(q * qm, k * km), a)
            upper = (rowi & half) != 0
            t3 = tot.reshape(t // 8, 8, HG_DK)
            sib = pltpu.roll(t3, half, 1).reshape(t, HG_DK)
            if half != 4:
                sib = jnp.where(upper, sib, pltpu.roll(t3, 8 - half, 1).reshape(t, HG_DK))
            pfx = pfx * jnp.where(upper, sib, 1.0)
            sfx = sfx * jnp.where(upper, 1.0, sib)
            tot = tot * sib
        else:
            nb = t // (2 * half)

            def sp(x):
                return x.reshape(nb, 2, half, x.shape[-1])

            def jn(lo, hi):
                return jnp.concatenate([lo[:, None], hi[:, None]], axis=1).reshape(t, lo.shape[-1])

            qrows = (sp(q)[:, qside] * sp(qm)[:, qside]).reshape(t // 2, HG_DK)
            blk = mm_nt(qrows, k * km).reshape(nb, half, t)
            a4 = sp(a)
            sel = jnp.where(sp(code)[:, qside] == lvl, blk, a4[:, qside])
            a = jn(a4[:, 0], sel) if fwd else jn(sel, a4[:, 1])
            t4, p4, s4 = sp(tot), sp(pfx), sp(sfx)
            pfx = jn(p4[:, 0], p4[:, 1] * t4[:, 0])
            sfx = jn(s4[:, 0] * t4[:, 1], s4[:, 1])
            tt = t4[:, 0] * t4[:, 1]
            tot = jn(tt, tt)
        half *= 2
        lvl += 1
    qm, km = (pfx, sfx) if fwd else (sfx, pfx)
    st = s_ref[sidx]
    inter = lax.dot_general((q * qm).astype(BF16), st.astype(BF16), nt, preferred_element_type=F32)
    intra = jnp.dot(a.astype(BF16), v, preferred_element_type=F32)
    o_ref[:, hs] = inter + intra
    upd = lax.dot_general(v, (k * km).astype(BF16), (((0,), (0,)), ((), ())),
                          preferred_element_type=F32)
    s_ref[sidx] = st * tot[0:1, :] + upd


def _hg_kernel(rf_ref, rb_ref, qf_ref, hf_ref, vf_ref, qb_ref, hb_ref, vb_ref, wf_ref, wb_ref, lb_ref,
               of_ref, ob_ref, s_ref):
    c = pl.program_id(0)

    @pl.when(rf_ref[c] == 1)
    def _():
        s_ref[0:HG_HEADS] = jnp.zeros((HG_HEADS, HG_DK, HG_DK), F32)

    @pl.when(rb_ref[c] == 1)
    def _():
        s_ref[HG_HEADS:2 * HG_HEADS] = jnp.zeros((HG_HEADS, HG_DK, HG_DK), F32)

    t = HG_T
    row = lax.broadcasted_iota(jnp.int32, (t, t), 0)
    col = lax.broadcasted_iota(jnp.int32, (t, t), 1)
    x = row ^ col
    hb = (pltpu.bitcast(x.astype(F32), jnp.int32) >> 23) - 127
    diag = jnp.where(row == col, -1, -2)
    code_f = jnp.where(row > col, hb, diag)
    code_b = jnp.where(row < col, hb, diag)
    rowi = lax.broadcasted_iota(jnp.int32, (t, HG_DK), 0)
    for hp in range(HG_HEADS // 2):
        ps = slice(2 * hp * HG_DK, (2 * hp + 2) * HG_DK)
        zf = jnp.dot(hf_ref[...], wf_ref[:, ps], preferred_element_type=F32)
        zb = jnp.dot(hb_ref[...], wb_ref[:, ps], preferred_element_type=F32)
        for i in range(2):
            h = 2 * hp + i
            hs = slice(h * HG_DK, (h + 1) * HG_DK)
            zs = slice(i * HG_DK, (i + 1) * HG_DK)
            _hg_unit(qf_ref, zf[:, zs], vf_ref, lb_ref[0:1, hs], of_ref, s_ref, h, hs, code_f, rowi,
                     True)
            _hg_unit(qb_ref, zb[:, zs], vb_ref, lb_ref[1:2, hs], ob_ref, s_ref, HG_HEADS + h, hs,
                     code_b, rowi, False)


def _hgrn(p16, hn, w_in, lb, reset_f, reset_b, layer):
    m = p16.shape[0]
    n = m // HG_T
    wcol = FGATE_COL0 // HG_DIM

    def fspec(col):
        return pl.BlockSpec((HG_T, HG_DIM), lambda c, a, b: (c, col))

    def bspec(col):
        return pl.BlockSpec((HG_T, HG_DIM), lambda c, a, b: (n - 1 - c, col))

    def wspec(col):
        return pl.BlockSpec((None, D_MODEL, HG_DIM), lambda c, a, b: (layer, 0, col),
                            pipeline_mode=pl.Buffered(1))

    grid_spec = pltpu.PrefetchScalarGridSpec(
        num_scalar_prefetch=2,
        grid=(n,),
        in_specs=[
            fspec(COL768["cq"]), pl.BlockSpec((HG_T, D_MODEL), lambda c, a, b: (c, 0)),
            fspec(COL768["ci"]),
            bspec(COL768["cq"]), pl.BlockSpec((HG_T, D_MODEL), lambda c, a, b: (n - 1 - c, 0)),
            bspec(COL768["ci"]),
            wspec(wcol), wspec(wcol + 1),
            pl.BlockSpec((None, 2, HG_DIM), lambda c, a, b: (layer, 0, 0)),
        ],
        out_specs=[
            pl.BlockSpec((HG_T, HG_DIM), lambda c, a, b: (c, 0)),
            pl.BlockSpec((HG_T, HG_DIM), lambda c, a, b: (n - 1 - c, 0)),
        ],
        scratch_shapes=[pltpu.VMEM((2 * HG_HEADS, HG_DK, HG_DK), F32)],
    )
    return pl.pallas_call(
        _hg_kernel,
        grid_spec=grid_spec,
        out_shape=[jax.ShapeDtypeStruct((m, HG_DIM), F32), jax.ShapeDtypeStruct((m, HG_DIM), F32)],
        compiler_params=_cparams(("arbitrary",)),
        name="hgrn2",
    )(reset_f, reset_b, p16, hn, p16, p16, hn, p16, w_in, w_in, lb)


def _mix_kernel(st_ref, en_ref, x_ref, ah_ref, ab_ref, ac_ref, ahp_ref, acp_ref, ahn_ref, acn_ref,
                cw_ref, yb_ref, of_ref, ob_ref, cg_ref, ng_ref, ga_ref, gb_ref, gc_ref,
                wa_ref, wb_ref, wc_ref, wo_ref, o_ref):
    i = pl.program_id(0)
    tm = x_ref.shape[0]
    u = ac_ref[...].astype(F32) * ah_ref[...].astype(F32)
    keep_p = jnp.where(st_ref[i] == 1, 0.0, 1.0)
    keep_n = jnp.where(en_ref[i] == 1, 0.0, 1.0)
    h = CONV_HALO
    up_edge = acp_ref[h - 1:h, :].astype(F32) * ahp_ref[h - 1:h, :].astype(F32) * keep_p
    un_edge = acn_ref[0:1, :].astype(F32) * ahn_ref[0:1, :].astype(F32) * keep_n
    rowi = lax.broadcasted_iota(jnp.int32, u.shape, 0)
    u_prev = jnp.where(rowi == 0, up_edge, pltpu.roll(u, 1, 0))
    u_next = jnp.where(rowi == tm - 1, un_edge, pltpu.roll(u, tm - 1, 0))
    cw = cw_ref[...]
    y_a = ab_ref[...].astype(F32) * (u_prev * cw[0:1] + u * cw[1:2] + u_next * cw[2:3])
    cg = cg_ref[...].astype(F32)
    gate = cg * _sigmoid(cg)
    ng = ng_ref[...]
    parts = []
    for hd in range(HG_HEADS):
        hs = slice(hd * HG_DK, (hd + 1) * HG_DK)
        o = of_ref[:, hs] + ob_ref[:, hs]
        o = o * lax.rsqrt(jnp.mean(o * o, axis=-1, keepdims=True) + EPS)
        parts.append(o * ng[:, hs] * gate[:, hs])
    y_c = jnp.concatenate(parts, axis=-1)
    mix = _sigmoid(ga_ref[...].astype(F32)) * jnp.dot(y_a.astype(BF16), wa_ref[...],
                                                      preferred_element_type=F32)
    mix = mix + _sigmoid(gb_ref[...].astype(F32)) * jnp.dot(yb_ref[...], wb_ref[...],
                                                            preferred_element_type=F32)
    mix = mix + _sigmoid(gc_ref[...].astype(F32)) * jnp.dot(y_c.astype(BF16), wc_ref[...],
                                                            preferred_element_type=F32)
    o_ref[...] = x_ref[...] + jnp.dot(mix.astype(BF16), wo_ref[...], preferred_element_type=F32)


def _mix(x, p16, y_b, o_f, o_b, conv_w, ng, wa, wb, wc, wo, is_start, is_end, layer):
    m = x.shape[0]
    tm = MIX_TM
    hb = tm // CONV_HALO
    nh = m // CONV_HALO

    def tok(width, col):
        return pl.BlockSpec((tm, width), lambda i, a, b: (i, col))

    def halo_prev(col):
        return pl.BlockSpec((CONV_HALO, CONV_DIM), lambda i, a, b: (jnp.maximum(i * hb - 1, 0), col))

    def halo_next(col):
        return pl.BlockSpec((CONV_HALO, CONV_DIM),
                            lambda i, a, b: (jnp.minimum((i + 1) * hb, nh - 1), col))

    def weight(rows):
        return pl.BlockSpec((None, rows, D_MODEL), lambda i, a, b: (layer, 0, 0),
                            pipeline_mode=pl.Buffered(1))

    grid_spec = pltpu.PrefetchScalarGridSpec(
        num_scalar_prefetch=2,
        grid=(m // tm,),
        in_specs=[
            tok(D_MODEL, 0),
            tok(CONV_DIM, COL512["a_h"]), tok(CONV_DIM, COL512["a_b"]), tok(CONV_DIM, COL512["a_c"]),
            halo_prev(COL512["a_h"]), halo_prev(COL512["a_c"]),
            halo_next(COL512["a_h"]), halo_next(COL512["a_c"]),
            pl.BlockSpec((None, 3, CONV_DIM), lambda i, a, b: (layer, 0, 0)),
            tok(NA_DIM, 0), tok(HG_DIM, 0), tok(HG_DIM, 0), tok(HG_DIM, COL768["cg"]),
            pl.BlockSpec((None, 1, HG_DIM), lambda i, a, b: (layer, 0, 0)),
            tok(D_MODEL, COL2048["ga"]), tok(D_MODEL, COL2048["gb"]), tok(D_MODEL, COL2048["gc"]),
            weight(CONV_DIM), weight(NA_DIM), weight(HG_DIM), weight(D_MODEL),
        ],
        out_specs=pl.BlockSpec((tm, D_MODEL), lambda i, a, b: (i, 0)),
    )
    return pl.pallas_call(
        _mix_kernel,
        grid_spec=grid_spec,
        out_shape=jax.ShapeDtypeStruct((m, D_MODEL), F32),
        compiler_params=_cparams(("parallel",)),
        name="mix",
    )(is_start, is_end, x, p16, p16, p16, p16, p16, p16, p16, conv_w, y_b, o_f, o_b, p16, ng,
      p16, p16, p16, wa, wb, wc, wo)


def _ffn_kernel(x_ref, g_ref, wg_ref, wu_ref, wd_ref, fg_ref, o_ref, h_ref, *, final):
    j = pl.program_id(1)

    @pl.when(j == 0)
    def _():
        x = x_ref[...]
        h_ref[...] = _rms_scale(x, g_ref[...]).astype(BF16)
        o_ref[...] = x

    h = h_ref[...]
    g = jnp.dot(h, wg_ref[...], preferred_element_type=F32)
    u = jnp.dot(h, wu_ref[...], preferred_element_type=F32)
    a = (g * _sigmoid(g) * u).astype(BF16)
    o_ref[...] += jnp.dot(a, wd_ref[...], preferred_element_type=F32)

    if final:
        @pl.when(j == pl.num_programs(1) - 1)
        def _():
            o_ref[...] = _rms_scale(o_ref[...], fg_ref[...])


def _ffn(x, g, wg, wu, wd, fg, layer, tok_start, tok_len, final):
    tm = next(c for c in FFN_TM_CHOICES if tok_start % c == 0 and tok_len % c == 0)
    off = tok_start // tm
    return pl.pallas_call(
        functools.partial(_ffn_kernel, final=final),
        grid=(tok_len // tm, D_FF // FFN_TF),
        in_specs=[
            pl.BlockSpec((tm, D_MODEL), lambda i, j: (i + off, 0)),
            pl.BlockSpec((None, 1, D_MODEL), lambda i, j: (layer, 0, 0)),
            pl.BlockSpec((None, D_MODEL, FFN_TF), lambda i, j: (layer, 0, j)),
            pl.BlockSpec((None, D_MODEL, FFN_TF), lambda i, j: (layer, 0, j)),
            pl.BlockSpec((None, FFN_TF, D_MODEL), lambda i, j: (layer, j, 0)),
            pl.BlockSpec((1, D_MODEL), lambda i, j: (0, 0)),
        ],
        out_specs=pl.BlockSpec((tm, D_MODEL), lambda i, j: (i, 0)),
        out_shape=jax.ShapeDtypeStruct((tok_len, D_MODEL), F32),
        scratch_shapes=[pltpu.VMEM((tm, D_MODEL), BF16)],
        compiler_params=_cparams(("parallel", "arbitrary")),
        name="ffn_final" if final else "ffn",
    )(x, g, wg, wu, wd, fg)


def _descriptors(seqs, m):
    na_blk = NA_ROWS * GRID_W
    s0 = np.zeros(m // na_blk, np.int32)
    s1 = np.zeros(m // na_blk, np.int32)
    reset_f = np.zeros(m // HG_T, np.int32)
    reset_b = np.zeros(m // HG_T, np.int32)
    is_start = np.zeros(m // MIX_TM, np.int32)
    is_end = np.zeros(m // MIX_TM, np.int32)
    n_chunks = m // HG_T
    for start, length in seqs:
        assert start % na_blk == 0 and length % na_blk == 0 and length // GRID_W >= 2 * WIN_R
        assert start % MIX_TM == 0 and length % MIX_TM == 0
        end = start + length
        s0[start // na_blk:end // na_blk] = start // GRID_W
        s1[start // na_blk:end // na_blk] = end // GRID_W
        reset_f[start // HG_T] = 1
        reset_b[n_chunks - 1 - (end // HG_T - 1)] = 1
        is_start[start // MIX_TM] = 1
        is_end[end // MIX_TM - 1] = 1
    return tuple(jnp.asarray(a) for a in (s0, s1, reset_f, reset_b, is_start, is_end))


def _trunk(x, seqs, group_tokens, norm1_g, w_in, conv_w, t_rel, lb_all, hg_norm_g, w_br_conv, w_br_attn,
           w_br_hgrn, w_mix_out, norm2_g, w_ffn_gate, w_ffn_up, w_ffn_down, final_g):
    m = x.shape[0]
    depth = w_in.shape[0]
    s0, s1, reset_f, reset_b, is_start, is_end = _descriptors(seqs, m)
    ffn_w = (norm2_g, w_ffn_gate, w_ffn_up, w_ffn_down, final_g)
    for l in range(depth):
        p16, hn = _inproj(x, norm1_g, w_in, l)
        y_b = _na(p16, t_rel, s0, s1, l)
        o_f, o_b = _hgrn(p16, hn, w_in, lb_all, reset_f, reset_b, l)
        x = _mix(x, p16, y_b, o_f, o_b, conv_w, hg_norm_g, w_br_conv, w_br_attn, w_br_hgrn,
                 w_mix_out, is_start, is_end, l)
        if l < depth - 1:
            x = _ffn(x, *ffn_w, l, 0, m, False)
    outs = []
    tok = 0
    for n in group_tokens:
        outs.append(_ffn(x, *ffn_w, depth - 1, tok, n, True))
        tok += n
    return outs


def _prepare_params(norm1_g, w_in, conv_w, rpb, hg_lower, hg_norm_g, w_br_conv, w_br_attn, w_br_hgrn,
                    w_mix_out, norm2_g, w_ffn_gate, w_ffn_up, w_ffn_down, final_g):
    sm = jax.nn.softmax(hg_lower.astype(F32), axis=0)
    lb_all = jnp.cumsum(sm, axis=0) - sm[0]
    bf = lambda w: w.astype(BF16)
    return (norm1_g[:, None, :].astype(F32), bf(w_in), conv_w.astype(F32), _na_bias_table(rpb), lb_all,
            hg_norm_g[:, None, :].astype(F32), bf(w_br_conv), bf(w_br_attn), bf(w_br_hgrn),
            bf(w_mix_out), norm2_g[:, None, :].astype(F32), bf(w_ffn_gate), bf(w_ffn_up),
            bf(w_ffn_down), final_g[None, :].astype(F32))


def kernel(x_prompt, x_sample, norm1_g, w_in, conv_w, rpb, hg_lower, hg_norm_g, w_br_conv, w_br_attn,
           w_br_hgrn, w_mix_out, norm2_g, w_ffn_gate, w_ffn_up, w_ffn_down, final_g):
    groups = [x_prompt, x_sample]
    seqs = []
    tok = 0
    for g in groups:
        b, length, _ = g.shape
        for _ in range(b):
            seqs.append((tok, length))
            tok += length
    x = jnp.concatenate([g.reshape(-1, D_MODEL) for g in groups], axis=0)
    params = _prepare_params(norm1_g, w_in, conv_w, rpb, hg_lower, hg_norm_g, w_br_conv, w_br_attn,
                             w_br_hgrn, w_mix_out, norm2_g, w_ffn_gate, w_ffn_up, w_ffn_down, final_g)
    outs = _trunk(x, seqs, [g.shape[0] * g.shape[1] for g in groups], *params)
    return tuple(o.reshape(g.shape) for o, g in zip(outs, groups))
```

```python
import functools

import numpy as np
import jax
import jax.numpy as jnp
from jax import lax
from jax.experimental import pallas as pl
from jax.experimental.pallas import tpu as pltpu

F32 = jnp.float32
BF16 = jnp.bfloat16

D_MODEL = 2048
CONV_DIM = 512
NA_HEADS = 12
NA_HEAD_DIM = 64
NA_DIM = NA_HEADS * NA_HEAD_DIM
GRID_W = 64
WIN_R = 8
WIN_C = 16
HG_HEADS = 6
HG_DK = 128
HG_DIM = HG_HEADS * HG_DK
F_MIN = 1e-30
D_FF = 5632
EPS = 1e-6
NEG_INF = -1e30

P32_WIDTH = 2 * HG_DIM
P16_WIDTH = 3 * CONV_DIM + 3 * NA_DIM + 3 * HG_DIM + 3 * D_MODEL
FGATE_COL0 = 3 * CONV_DIM + 3 * NA_DIM + HG_DIM
COL768 = dict(nq=2, nk=3, nv=4, cq=5, ci=6, cg=7)
COL512 = dict(a_h=0, a_b=1, a_c=2)
COL2048 = dict(ga=3, gb=4, gc=5)

VMEM_LIMIT_BYTES = 56 * 1024 * 1024

IN_TM, IN_TN = 1024, 1536
FFN_TM_CHOICES, FFN_TF = (1024, 512), 512
MIX_TM = 256
NA_ROWS = 8
NA_SUB = 4
NA_TABLE_W = (2 * WIN_R - 1) * GRID_W
NA_SM_ROWS = 32
HG_T = 128
CONV_HALO = 16


def _cparams(sem):
    return pltpu.CompilerParams(dimension_semantics=sem, vmem_limit_bytes=VMEM_LIMIT_BYTES)


def _rms_scale(x, g):
    ms = jnp.mean(x * x, axis=-1, keepdims=True)
    return x * lax.rsqrt(ms + EPS) * g


def _sigmoid(x):
    return 1.0 / (1.0 + jnp.exp(-x))


def _inproj_kernel(x_ref, g_ref, w_ref, o_ref, h_ref):
    @pl.when(pl.program_id(1) == 0)
    def _():
        h_ref[...] = _rms_scale(x_ref[...], g_ref[...]).astype(BF16)

    o_ref[...] = jnp.dot(h_ref[...], w_ref[...], preferred_element_type=F32).astype(BF16)


def _inproj(x, g, w, layer):
    m = x.shape[0]
    nj = P16_WIDTH // IN_TN
    skip = FGATE_COL0 // IN_TN
    nskip = P32_WIDTH // IN_TN
    return pl.pallas_call(
        _inproj_kernel,
        grid=(m // IN_TM, nj),
        in_specs=[
            pl.BlockSpec((IN_TM, D_MODEL), lambda i, j: (i, 0)),
            pl.BlockSpec((None, 1, D_MODEL), lambda i, j: (layer, 0, 0)),
            pl.BlockSpec((None, D_MODEL, IN_TN),
                         lambda i, j: (layer, 0, jnp.where(j >= skip, j + nskip, j))),
        ],
        out_specs=[
            pl.BlockSpec((IN_TM, IN_TN), lambda i, j: (i, j)),
            pl.BlockSpec((IN_TM, D_MODEL), lambda i, j: (i, 0)),
        ],
        out_shape=[jax.ShapeDtypeStruct((m, P16_WIDTH), BF16),
                   jax.ShapeDtypeStruct((m, D_MODEL), BF16)],
        compiler_params=_cparams(("parallel", "arbitrary")),
        name="inproj",
    )(x, g, w)


def _na_bias_row(t_ref, head, a):
    nk = (NA_SUB + WIN_R) * GRID_W
    shift = NA_SUB - 1 - a
    off = (shift // 2) * 2 * GRID_W
    return t_ref[head, shift % 2, :, off:off + nk]


def _na_kernel(s0_ref, s1_ref, q_ref, kp_ref, kc_ref, kn_ref, vp_ref, vc_ref, vn_ref, t_ref, o_ref):
    i = pl.program_id(0)
    s0 = s0_ref[i]
    s1 = s1_ref[i]
    nq = NA_SUB * GRID_W
    nk = (NA_SUB + WIN_R) * GRID_W
    lane = lax.broadcasted_iota(jnp.int32, (nq, 2 * NA_HEAD_DIM), 1)
    lo_half = lane < NA_HEAD_DIM
    n_sub = NA_ROWS // NA_SUB
    tasks = [(sub, hp, half) for sub in range(n_sub) for hp in range(NA_HEADS // 2) for half in range(2)]

    def keys(ref_prev, ref_cur, ref_next, sub, cs):
        if sub == 0:
            return jnp.concatenate([ref_prev[:, cs], ref_cur[:, cs]], axis=0)
        return jnp.concatenate([ref_cur[:, cs], ref_next[:, cs]], axis=0)

    def scores(task):
        sub, hp, half = task
        cs = slice(hp * 2 * NA_HEAD_DIM, (hp + 1) * 2 * NA_HEAD_DIM)
        q2 = q_ref[sub * nq:(sub + 1) * nq, cs] * (NA_HEAD_DIM ** -0.5)
        keep = lo_half if half == 0 else jnp.logical_not(lo_half)
        qh = jnp.where(keep, q2, jnp.zeros_like(q2))
        k2 = keys(kp_ref, kc_ref, kn_ref, sub, cs)
        return lax.dot_general(qh, k2, (((1,), (1,)), ((), ())), preferred_element_type=F32)

    rowmasks = []
    for sub in range(n_sub):
        qbase = NA_ROWS * i + NA_SUB * sub
        kbase = NA_ROWS * i - NA_SUB if sub == 0 else NA_ROWS * i
        qrow = qbase + lax.broadcasted_iota(jnp.int32, (nq, nk), 0) // GRID_W
        krow = kbase + lax.broadcasted_iota(jnp.int32, (nq, nk), 1) // GRID_W
        rs = jnp.clip(qrow - WIN_R // 2, s0, s1 - WIN_R)
        off = (krow - rs).astype(jnp.uint32)
        rowmasks.append(jnp.where(off < WIN_R, 0.0, NEG_INF).astype(F32))

    s_next = scores(tasks[0])
    outs = []
    for n, (sub, hp, half) in enumerate(tasks):
        s = s_next
        if n + 1 < len(tasks):
            s_next = scores(tasks[n + 1])
        cs = slice(hp * 2 * NA_HEAD_DIM, (hp + 1) * 2 * NA_HEAD_DIM)
        es, dens = [], []
        for a in range(NA_SUB):
            bias = _na_bias_row(t_ref, 2 * hp + half, a)
            for r0 in range(0, GRID_W, NA_SM_ROWS):
                rows = slice(a * GRID_W + r0, a * GRID_W + r0 + NA_SM_ROWS)
                sa = s[rows] + bias[r0:r0 + NA_SM_ROWS] + rowmasks[sub][rows]
                mx = jnp.max(sa, axis=-1, keepdims=True)
                ea = jnp.exp(sa - mx)
                dens.append(jnp.sum(ea, axis=-1, keepdims=True))
                es.append(ea.astype(BF16))
        e = jnp.concatenate(es, axis=0)
        den = jnp.concatenate(dens, axis=0)
        v2 = keys(vp_ref, vc_ref, vn_ref, sub, cs)
        o = jnp.dot(e, v2, preferred_element_type=F32)
        outs.append(o * (1.0 / den))
        if half == 1:
            o_ref[sub * nq:(sub + 1) * nq, cs] = jnp.where(lo_half, outs[0], outs[1]).astype(BF16)
            outs = []


def _na(p16, t_rel, s0, s1, layer):
    m = p16.shape[0]
    blk = NA_ROWS * GRID_W
    sub = NA_SUB * GRID_W
    nblk = m // blk
    nsub = m // sub
    r = NA_ROWS // NA_SUB

    def cur(col):
        return pl.BlockSpec((blk, NA_DIM), lambda i, a, b: (i, col))

    def prev(col):
        return pl.BlockSpec((sub, NA_DIM), lambda i, a, b: (jnp.maximum(r * i - 1, 0), col))

    def nxt(col):
        return pl.BlockSpec((sub, NA_DIM), lambda i, a, b: (jnp.minimum(r * i + r, nsub - 1), col))

    grid_spec = pltpu.PrefetchScalarGridSpec(
        num_scalar_prefetch=2,
        grid=(nblk,),
        in_specs=[
            cur(COL768["nq"]),
            prev(COL768["nk"]), cur(COL768["nk"]), nxt(COL768["nk"]),
            prev(COL768["nv"]), cur(COL768["nv"]), nxt(COL768["nv"]),
            pl.BlockSpec((None, NA_HEADS, 2, GRID_W, NA_TABLE_W), lambda i, a, b: (layer, 0, 0, 0, 0),
                         pipeline_mode=pl.Buffered(1)),
        ],
        out_specs=pl.BlockSpec((blk, NA_DIM), lambda i, a, b: (i, 0)),
    )
    return pl.pallas_call(
        _na_kernel,
        grid_spec=grid_spec,
        out_shape=jax.ShapeDtypeStruct((m, NA_DIM), BF16),
        compiler_params=_cparams(("parallel",)),
        name="natten",
    )(s0, s1, p16, p16, p16, p16, p16, p16, p16, t_rel)


def _na_bias_table(rpb):
    assert NA_SUB + WIN_R - 1 + NA_SUB - 1 == 2 * WIN_R - 2
    c = np.arange(GRID_W)
    cs = np.clip(c - WIN_C // 2, 0, GRID_W - WIN_C)
    kc = np.arange(GRID_W)[None, :]
    col_ok = (kc >= cs[:, None]) & (kc < cs[:, None] + WIN_C)
    dc = np.clip(kc - c[:, None] + WIN_C - 1, 0, 2 * WIN_C - 2)
    small = jnp.where(jnp.asarray(col_ok), rpb[..., jnp.asarray(dc)].astype(F32), NEG_INF)
    d, h, ndr = rpb.shape[0], rpb.shape[1], rpb.shape[2]
    flat = small.transpose(0, 1, 3, 2, 4).reshape(d, h, GRID_W, ndr * GRID_W)
    shifted = jnp.pad(flat[..., GRID_W:], ((0, 0), (0, 0), (0, 0), (0, GRID_W)))
    return jnp.stack([flat, shifted], axis=2)


def _hg_mm_nt(x, y):
    return lax.dot_general(x.astype(BF16), y.astype(BF16), (((1,), (1,)), ((), ())),
                           preferred_element_type=F32)


def _hg_prep(q_ref, z, v_ref, lb, hs, code, fwd):
    f = lb + (1.0 - lb) * _sigmoid(z)
    fc = jnp.maximum(f, F_MIN)
    k = 1.0 - f
    qv = q_ref[:, hs].astype(F32)
    q = qv * _sigmoid(qv)
    ones = jnp.ones_like(fc)
    pfx, sfx = (fc, ones) if fwd else (ones, fc)
    a = jnp.where(code == -1, _hg_mm_nt(q, k), 0.0)
    return dict(q=q, k=k, v=v_ref[:, hs], pfx=pfx, sfx=sfx, tot=fc, a=a, code=code, fwd=fwd, hs=hs)


def _hg_level(u, lvl, rowi):
    t = HG_T
    half = 1 << lvl
    fwd, q, k, a, code = u["fwd"], u["q"], u["k"], u["a"], u["code"]
    pfx, sfx, tot = u["pfx"], u["sfx"], u["tot"]
    qm, km = (pfx, sfx) if fwd else (sfx, pfx)
    if half < 8:
        a = jnp.where(code == lvl, _hg_mm_nt(q * qm, k * km), a)
        upper = (rowi & half) != 0
        t3 = tot.reshape(t // 8, 8, HG_DK)
        sib = pltpu.roll(t3, half, 1).reshape(t, HG_DK)
        if half != 4:
            sib = jnp.where(upper, sib, pltpu.roll(t3, 8 - half, 1).reshape(t, HG_DK))
        pfx = pfx * jnp.where(upper, sib, 1.0)
        sfx = sfx * jnp.where(upper, 1.0, sib)
        tot = tot * sib
    else:
        nb = t // (2 * half)
        qside = 1 if fwd else 0

        def sp(x):
            return x.reshape(nb, 2, half, x.shape[-1])

        def jn(lo, hi):
            return jnp.concatenate([lo[:, None], hi[:, None]], axis=1).reshape(t, lo.shape[-1])

        qrows = (sp(q)[:, qside] * sp(qm)[:, qside]).reshape(t // 2, HG_DK)
        blk = _hg_mm_nt(qrows, k * km).reshape(nb, half, t)
        a4 = sp(a)
        sel = jnp.where(sp(code)[:, qside] == lvl, blk, a4[:, qside])
        a = jn(a4[:, 0], sel) if fwd else jn(sel, a4[:, 1])
        t4, p4, s4 = sp(tot), sp(pfx), sp(sfx)
        pfx = jn(p4[:, 0], p4[:, 1] * t4[:, 0])
        sfx = jn(s4[:, 0] * t4[:, 1], s4[:, 1])
        tt = t4[:, 0] * t4[:, 1]
        tot = jn(tt, tt)
    u.update(a=a, pfx=pfx, sfx=sfx, tot=tot)


def _hg_finish_levels(u):
    qm, km = (u["pfx"], u["sfx"]) if u["fwd"] else (u["sfx"], u["pfx"])
    return dict(qd=(u["q"] * qm).astype(BF16), kd=(u["k"] * km).astype(BF16), a=u["a"].astype(BF16),
                v=u["v"], tot=u["tot"][0:1, :], hs=u["hs"])


def _hg_tail(u, o_ref, s_ref, sidx):
    st = s_ref[sidx]
    inter = lax.dot_general(u["qd"], st.astype(BF16), (((1,), (1,)), ((), ())),
                            preferred_element_type=F32)
    intra = jnp.dot(u["a"], u["v"], preferred_element_type=F32)
    o_ref[:, u["hs"]] = inter + intra
    upd = lax.dot_general(u["v"], u["kd"], (((0,), (0,)), ((), ())), preferred_element_type=F32)
    s_ref[sidx] = st * u["tot"] + upd


def _hg_kernel(rf_ref, rb_ref, qf_ref, hf_ref, vf_ref, qb_ref, hb_ref, vb_ref, wf_ref, wb_ref, lb_ref,
               of_ref, ob_ref, s_ref):
    c = pl.program_id(0)

    @pl.when(rf_ref[c] == 1)
    def _():
        s_ref[0:HG_HEADS] = jnp.zeros((HG_HEADS, HG_DK, HG_DK), F32)

    @pl.when(rb_ref[c] == 1)
    def _():
        s_ref[HG_HEADS:2 * HG_HEADS] = jnp.zeros((HG_HEADS, HG_DK, HG_DK), F32)

    t = HG_T
    row = lax.broadcasted_iota(jnp.int32, (t, t), 0)
    col = lax.broadcasted_iota(jnp.int32, (t, t), 1)
    x = row ^ col
    hb = (pltpu.bitcast(x.astype(F32), jnp.int32) >> 23) - 127
    diag = jnp.where(row == col, -1, -2)
    code_f = jnp.where(row > col, hb, diag)
    code_b = jnp.where(row < col, hb, diag)
    rowi = lax.broadcasted_iota(jnp.int32, (t, HG_DK), 0)
    n_lvl = HG_T.bit_length() - 1
    n_pairs = HG_HEADS // 2

    def project(hp):
        ps = slice(2 * hp * HG_DK, (2 * hp + 2) * HG_DK)
        return (jnp.dot(hf_ref[...], wf_ref[:, ps], preferred_element_type=F32),
                jnp.dot(hb_ref[...], wb_ref[:, ps], preferred_element_type=F32))

    pending = []
    z_next = project(0)
    for hp in range(n_pairs):
        zf, zb = z_next
        units = []
        for i in range(2):
            h = 2 * hp + i
            hs = slice(h * HG_DK, (h + 1) * HG_DK)
            zs = slice(i * HG_DK, (i + 1) * HG_DK)
            units.append((_hg_prep(qf_ref, zf[:, zs], vf_ref, lb_ref[0:1, hs], hs, code_f, True),
                          of_ref, h))
            units.append((_hg_prep(qb_ref, zb[:, zs], vb_ref, lb_ref[1:2, hs], hs, code_b, False),
                          ob_ref, HG_HEADS + h))
        if hp + 1 < n_pairs:
            z_next = project(hp + 1)
        for lvl in range(n_lvl):
            for u, _, _ in units:
                _hg_level(u, lvl, rowi)
        done = [(_hg_finish_levels(u), o_ref, sidx) for u, o_ref, sidx in units]
        for u, o_ref, sidx in pending:
            _hg_tail(u, o_ref, s_ref, sidx)
        pending = done
    for u, o_ref, sidx in pending:
        _hg_tail(u, o_ref, s_ref, sidx)


def _hgrn(p16, hn, w_in, lb, reset_f, reset_b, layer):
    m = p16.shape[0]
    n = m // HG_T
    wcol = FGATE_COL0 // HG_DIM

    def fspec(col):
        return pl.BlockSpec((HG_T, HG_DIM), lambda c, a, b: (c, col))

    def bspec(col):
        return pl.BlockSpec((HG_T, HG_DIM), lambda c, a, b: (n - 1 - c, col))

    def wspec(col):
        return pl.BlockSpec((None, D_MODEL, HG_DIM), lambda c, a, b: (layer, 0, col),
                            pipeline_mode=pl.Buffered(1))

    grid_spec = pltpu.PrefetchScalarGridSpec(
        num_scalar_prefetch=2,
        grid=(n,),
        in_specs=[
            fspec(COL768["cq"]), pl.BlockSpec((HG_T, D_MODEL), lambda c, a, b: (c, 0)),
            fspec(COL768["ci"]),
            bspec(COL768["cq"]), pl.BlockSpec((HG_T, D_MODEL), lambda c, a, b: (n - 1 - c, 0)),
            bspec(COL768["ci"]),
            wspec(wcol), wspec(wcol + 1),
            pl.BlockSpec((None, 2, HG_DIM), lambda c, a, b: (layer, 0, 0)),
        ],
        out_specs=[
            pl.BlockSpec((HG_T, HG_DIM), lambda c, a, b: (c, 0)),
            pl.BlockSpec((HG_T, HG_DIM), lambda c, a, b: (n - 1 - c, 0)),
        ],
        scratch_shapes=[pltpu.VMEM((2 * HG_HEADS, HG_DK, HG_DK), F32)],
    )
    return pl.pallas_call(
        _hg_kernel,
        grid_spec=grid_spec,
        out_shape=[jax.ShapeDtypeStruct((m, HG_DIM), F32), jax.ShapeDtypeStruct((m, HG_DIM), F32)],
        compiler_params=_cparams(("arbitrary",)),
        name="hgrn2",
    )(reset_f, reset_b, p16, hn, p16, p16, hn, p16, w_in, w_in, lb)


def _mix_kernel(st_ref, en_ref, x_ref, ah_ref, ab_ref, ac_ref, ahp_ref, acp_ref, ahn_ref, acn_ref,
                cw_ref, yb_ref, of_ref, ob_ref, cg_ref, ng_ref, ga_ref, gb_ref, gc_ref,
                wa_ref, wb_ref, wc_ref, wo_ref, o_ref):
    i = pl.program_id(0)
    tm = x_ref.shape[0]
    u = ac_ref[...].astype(F32) * ah_ref[...].astype(F32)
    keep_p = jnp.where(st_ref[i] == 1, 0.0, 1.0)
    keep_n = jnp.where(en_ref[i] == 1, 0.0, 1.0)
    h = CONV_HALO
    up_edge = acp_ref[h - 1:h, :].astype(F32) * ahp_ref[h - 1:h, :].astype(F32) * keep_p
    un_edge = acn_ref[0:1, :].astype(F32) * ahn_ref[0:1, :].astype(F32) * keep_n
    rowi = lax.broadcasted_iota(jnp.int32, u.shape, 0)
    u_prev = jnp.where(rowi == 0, up_edge, pltpu.roll(u, 1, 0))
    u_next = jnp.where(rowi == tm - 1, un_edge, pltpu.roll(u, tm - 1, 0))
    cw = cw_ref[...]
    y_a = ab_ref[...].astype(F32) * (u_prev * cw[0:1] + u * cw[1:2] + u_next * cw[2:3])
    cg = cg_ref[...].astype(F32)
    gate = cg * _sigmoid(cg)
    ng = ng_ref[...]
    parts = []
    for hd in range(HG_HEADS):
        hs = slice(hd * HG_DK, (hd + 1) * HG_DK)
        o = of_ref[:, hs] + ob_ref[:, hs]
        o = o * lax.rsqrt(jnp.mean(o * o, axis=-1, keepdims=True) + EPS)
        parts.append(o * ng[:, hs] * gate[:, hs])
    y_c = jnp.concatenate(parts, axis=-1)
    mix = _sigmoid(ga_ref[...].astype(F32)) * jnp.dot(y_a.astype(BF16), wa_ref[...],
                                                      preferred_element_type=F32)
    mix = mix + _sigmoid(gb_ref[...].astype(F32)) * jnp.dot(yb_ref[...], wb_ref[...],
                                                            preferred_element_type=F32)
    mix = mix + _sigmoid(gc_ref[...].astype(F32)) * jnp.dot(y_c.astype(BF16), wc_ref[...],
                                                            preferred_element_type=F32)
    o_ref[...] = x_ref[...] + jnp.dot(mix.astype(BF16), wo_ref[...], preferred_element_type=F32)


def _mix(x, p16, y_b, o_f, o_b, conv_w, ng, wa, wb, wc, wo, is_start, is_end, layer):
    m = x.shape[0]
    tm = MIX_TM
    hb = tm // CONV_HALO
    nh = m // CONV_HALO

    def tok(width, col):
        return pl.BlockSpec((tm, width), lambda i, a, b: (i, col))

    def halo_prev(col):
        return pl.BlockSpec((CONV_HALO, CONV_DIM), lambda i, a, b: (jnp.maximum(i * hb - 1, 0), col))

    def halo_next(col):
        return pl.BlockSpec((CONV_HALO, CONV_DIM),
                            lambda i, a, b: (jnp.minimum((i + 1) * hb, nh - 1), col))

    def weight(rows):
        return pl.BlockSpec((None, rows, D_MODEL), lambda i, a, b: (layer, 0, 0),
                            pipeline_mode=pl.Buffered(1))

    grid_spec = pltpu.PrefetchScalarGridSpec(
        num_scalar_prefetch=2,
        grid=(m // tm,),
        in_specs=[
            tok(D_MODEL, 0),
            tok(CONV_DIM, COL512["a_h"]), tok(CONV_DIM, COL512["a_b"]), tok(CONV_DIM, COL512["a_c"]),
            halo_prev(COL512["a_h"]), halo_prev(COL512["a_c"]),
            halo_next(COL512["a_h"]), halo_next(COL512["a_c"]),
            pl.BlockSpec((None, 3, CONV_DIM), lambda i, a, b: (layer, 0, 0)),
            tok(NA_DIM, 0), tok(HG_DIM, 0), tok(HG_DIM, 0), tok(HG_DIM, COL768["cg"]),
            pl.BlockSpec((None, 1, HG_DIM), lambda i, a, b: (layer, 0, 0)),
            tok(D_MODEL, COL2048["ga"]), tok(D_MODEL, COL2048["gb"]), tok(D_MODEL, COL2048["gc"]),
            weight(CONV_DIM), weight(NA_DIM), weight(HG_DIM), weight(D_MODEL),
        ],
        out_specs=pl.BlockSpec((tm, D_MODEL), lambda i, a, b: (i, 0)),
    )
    return pl.pallas_call(
        _mix_kernel,
        grid_spec=grid_spec,
        out_shape=jax.ShapeDtypeStruct((m, D_MODEL), F32),
        compiler_params=_cparams(("parallel",)),
        name="mix",
    )(is_start, is_end, x, p16, p16, p16, p16, p16, p16, p16, conv_w, y_b, o_f, o_b, p16, ng,
      p16, p16, p16, wa, wb, wc, wo)


def _ffn_kernel(x_ref, g_ref, wg_ref, wu_ref, wd_ref, fg_ref, o_ref, h_ref, *, final):
    j = pl.program_id(1)

    @pl.when(j == 0)
    def _():
        x = x_ref[...]
        h_ref[...] = _rms_scale(x, g_ref[...]).astype(BF16)
        o_ref[...] = x

    h = h_ref[...]
    g = jnp.dot(h, wg_ref[...], preferred_element_type=F32)
    u = jnp.dot(h, wu_ref[...], preferred_element_type=F32)
    a = (g * _sigmoid(g) * u).astype(BF16)
    o_ref[...] += jnp.dot(a, wd_ref[...], preferred_element_type=F32)

    if final:
        @pl.when(j == pl.num_programs(1) - 1)
        def _():
            o_ref[...] = _rms_scale(o_ref[...], fg_ref[...])


def _ffn(x, g, wg, wu, wd, fg, layer, tok_start, tok_len, final):
    tm = next(c for c in FFN_TM_CHOICES if tok_start % c == 0 and tok_len % c == 0)
    off = tok_start // tm
    return pl.pallas_call(
        functools.partial(_ffn_kernel, final=final),
        grid=(tok_len // tm, D_FF // FFN_TF),
        in_specs=[
            pl.BlockSpec((tm, D_MODEL), lambda i, j: (i + off, 0)),
            pl.BlockSpec((None, 1, D_MODEL), lambda i, j: (layer, 0, 0)),
            pl.BlockSpec((None, D_MODEL, FFN_TF), lambda i, j: (layer, 0, j)),
            pl.BlockSpec((None, D_MODEL, FFN_TF), lambda i, j: (layer, 0, j)),
            pl.BlockSpec((None, FFN_TF, D_MODEL), lambda i, j: (layer, j, 0)),
            pl.BlockSpec((1, D_MODEL), lambda i, j: (0, 0)),
        ],
        out_specs=pl.BlockSpec((tm, D_MODEL), lambda i, j: (i, 0)),
        out_shape=jax.ShapeDtypeStruct((tok_len, D_MODEL), F32),
        scratch_shapes=[pltpu.VMEM((tm, D_MODEL), BF16)],
        compiler_params=_cparams(("parallel", "arbitrary")),
        name="ffn_final" if final else "ffn",
    )(x, g, wg, wu, wd, fg)


def _descriptors(seqs, m):
    na_blk = NA_ROWS * GRID_W
    s0 = np.zeros(m // na_blk, np.int32)
    s1 = np.zeros(m // na_blk, np.int32)
    reset_f = np.zeros(m // HG_T, np.int32)
    reset_b = np.zeros(m // HG_T, np.int32)
    is_start = np.zeros(m // MIX_TM, np.int32)
    is_end = np.zeros(m // MIX_TM, np.int32)
    n_chunks = m // HG_T
    for start, length in seqs:
        assert start % na_blk == 0 and length % na_blk == 0 and length // GRID_W >= 2 * WIN_R
        assert start % MIX_TM == 0 and length % MIX_TM == 0
        end = start + length
        s0[start // na_blk:end // na_blk] = start // GRID_W
        s1[start // na_blk:end // na_blk] = end // GRID_W
        reset_f[start // HG_T] = 1
        reset_b[n_chunks - 1 - (end // HG_T - 1)] = 1
        is_start[start // MIX_TM] = 1
        is_end[end // MIX_TM - 1] = 1
    return tuple(jnp.asarray(a) for a in (s0, s1, reset_f, reset_b, is_start, is_end))


def _trunk(x, seqs, group_tokens, norm1_g, w_in, conv_w, t_rel, lb_all, hg_norm_g, w_br_conv, w_br_attn,
           w_br_hgrn, w_mix_out, norm2_g, w_ffn_gate, w_ffn_up, w_ffn_down, final_g):
    m = x.shape[0]
    depth = w_in.shape[0]
    s0, s1, reset_f, reset_b, is_start, is_end = _descriptors(seqs, m)
    ffn_w = (norm2_g, w_ffn_gate, w_ffn_up, w_ffn_down, final_g)
    for l in range(depth):
        p16, hn = _inproj(x, norm1_g, w_in, l)
        y_b = _na(p16, t_rel, s0, s1, l)
        o_f, o_b = _hgrn(p16, hn, w_in, lb_all, reset_f, reset_b, l)
        x = _mix(x, p16, y_b, o_f, o_b, conv_w, hg_norm_g, w_br_conv, w_br_attn, w_br_hgrn,
                 w_mix_out, is_start, is_end, l)
        if l < depth - 1:
            x = _ffn(x, *ffn_w, l, 0, m, False)
    outs = []
    tok = 0
    for n in group_tokens:
        outs.append(_ffn(x, *ffn_w, depth - 1, tok, n, True))
        tok += n
    return outs


def _prepare_params(norm1_g, w_in, conv_w, rpb, hg_lower, hg_norm_g, w_br_conv, w_br_attn, w_br_hgrn,
                    w_mix_out, norm2_g, w_ffn_gate, w_ffn_up, w_ffn_down, final_g):
    sm = jax.nn.softmax(hg_lower.astype(F32), axis=0)
    lb_all = jnp.cumsum(sm, axis=0) - sm[0]
    bf = lambda w: w.astype(BF16)
    return (norm1_g[:, None, :].astype(F32), bf(w_in), conv_w.astype(F32), _na_bias_table(rpb), lb_all,
            hg_norm_g[:, None, :].astype(F32), bf(w_br_conv), bf(w_br_attn), bf(w_br_hgrn),
            bf(w_mix_out), norm2_g[:, None, :].astype(F32), bf(w_ffn_gate), bf(w_ffn_up),
            bf(w_ffn_down), final_g[None, :].astype(F32))


def kernel(x_prompt, x_sample, norm1_g, w_in, conv_w, rpb, hg_lower, hg_norm_g, w_br_conv, w_br_attn,
           w_br_hgrn, w_mix_out, norm2_g, w_ffn_gate, w_ffn_up, w_ffn_down, final_g):
    groups = [x_prompt, x_sample]
    seqs = []
    tok = 0
    for g in groups:
        b, length, _ = g.shape
        for _ in range(b):
            seqs.append((tok, length))
            tok += length
    x = jnp.concatenate([g.reshape(-1, D_MODEL) for g in groups], axis=0)
    params = _prepare_params(norm1_g, w_in, conv_w, rpb, hg_lower, hg_norm_g, w_br_conv, w_br_attn,
                             w_br_hgrn, w_mix_out, norm2_g, w_ffn_gate, w_ffn_up, w_ffn_down, final_g)
    outs = _trunk(x, seqs, [g.shape[0] * g.shape[1] for g in groups], *params)
    return tuple(o.reshape(g.shape) for o, g in zip(outs, groups))
```

```python
import functools

import numpy as np
import jax
import jax.numpy as jnp
from jax import lax
from jax.experimental import pallas as pl
from jax.experimental.pallas import tpu as pltpu

F32 = jnp.float32
BF16 = jnp.bfloat16

D_MODEL = 2048
CONV_DIM = 512
NA_HEADS = 12
NA_HEAD_DIM = 64
NA_DIM = NA_HEADS * NA_HEAD_DIM
GRID_W = 64
WIN_R = 8
WIN_C = 16
HG_HEADS = 6
HG_DK = 128
HG_DIM = HG_HEADS * HG_DK
F_MIN = 1e-30
D_FF = 5632
EPS = 1e-6
NEG_INF = -1e30
LOG2E = 1.4426950408889634

P32_WIDTH = 2 * HG_DIM
P16_WIDTH = 3 * CONV_DIM + 3 * NA_DIM + 3 * HG_DIM + 3 * D_MODEL
FGATE_COL0 = 3 * CONV_DIM + 3 * NA_DIM + HG_DIM
COL768 = dict(nq=2, nk=3, nv=4, cq=5, ci=6, cg=7)
COL512 = dict(a_h=0, a_b=1, a_c=2)
COL2048 = dict(ga=3, gb=4, gc=5)

VMEM_LIMIT_BYTES = 56 * 1024 * 1024

IN_TM, IN_TN = 1024, 1536
FFN_TM_CHOICES, FFN_TF = (1024, 512), 512
MIX_TM = 256
NA_ROWS = 8
NA_SUB = 4
NA_TABLE_W = (2 * WIN_R - 1) * GRID_W
NA_SM_ROWS = 32
HG_T = 128
HG_GROUP_UNITS = 1
CONV_HALO = 16


def _cparams(sem):
    return pltpu.CompilerParams(dimension_semantics=sem, vmem_limit_bytes=VMEM_LIMIT_BYTES)


def _rms_scale(x, g):
    ms = jnp.mean(x * x, axis=-1, keepdims=True)
    return x * lax.rsqrt(ms + EPS) * g


def _sigmoid(x):
    return 1.0 / (1.0 + jnp.exp(-x))


def _inproj_kernel(x_ref, g_ref, w_ref, o_ref, h_ref):
    @pl.when(pl.program_id(1) == 0)
    def _():
        h_ref[...] = _rms_scale(x_ref[...], g_ref[...]).astype(BF16)

    o_ref[...] = jnp.dot(h_ref[...], w_ref[...], preferred_element_type=F32).astype(BF16)


def _inproj(x, g, w, layer):
    m = x.shape[0]
    nj = P16_WIDTH // IN_TN
    skip = FGATE_COL0 // IN_TN
    nskip = P32_WIDTH // IN_TN
    return pl.pallas_call(
        _inproj_kernel,
        grid=(m // IN_TM, nj),
        in_specs=[
            pl.BlockSpec((IN_TM, D_MODEL), lambda i, j: (i, 0)),
            pl.BlockSpec((None, 1, D_MODEL), lambda i, j: (layer, 0, 0)),
            pl.BlockSpec((None, D_MODEL, IN_TN),
                         lambda i, j: (layer, 0, jnp.where(j >= skip, j + nskip, j))),
        ],
        out_specs=[
            pl.BlockSpec((IN_TM, IN_TN), lambda i, j: (i, j)),
            pl.BlockSpec((IN_TM, D_MODEL), lambda i, j: (i, 0)),
        ],
        out_shape=[jax.ShapeDtypeStruct((m, P16_WIDTH), BF16),
                   jax.ShapeDtypeStruct((m, D_MODEL), BF16)],
        compiler_params=_cparams(("parallel", "arbitrary")),
        name="inproj",
    )(x, g, w)


def _na_bias_row(t_ref, head, a):
    nk = (NA_SUB + WIN_R) * GRID_W
    shift = NA_SUB - 1 - a
    off = (shift // 2) * 2 * GRID_W
    return t_ref[head, shift % 2, :, off:off + nk]


def _na_kernel(s0_ref, s1_ref, q_ref, kp_ref, kc_ref, kn_ref, vp_ref, vc_ref, vn_ref, t_ref, o_ref):
    i = pl.program_id(0)
    s0 = s0_ref[i]
    s1 = s1_ref[i]
    nq = NA_SUB * GRID_W
    nk = (NA_SUB + WIN_R) * GRID_W
    lane = lax.broadcasted_iota(jnp.int32, (nq, 2 * NA_HEAD_DIM), 1)
    lo_half = lane < NA_HEAD_DIM
    n_sub = NA_ROWS // NA_SUB
    tasks = [(sub, hp, half) for sub in range(n_sub) for hp in range(NA_HEADS // 2) for half in range(2)]

    def keys(ref_prev, ref_cur, ref_next, sub, cs):
        if sub == 0:
            return jnp.concatenate([ref_prev[:, cs], ref_cur[:, cs]], axis=0)
        return jnp.concatenate([ref_cur[:, cs], ref_next[:, cs]], axis=0)

    def scores(task):
        sub, hp, half = task
        cs = slice(hp * 2 * NA_HEAD_DIM, (hp + 1) * 2 * NA_HEAD_DIM)
        q2 = q_ref[sub * nq:(sub + 1) * nq, cs] * (NA_HEAD_DIM ** -0.5 * LOG2E)
        keep = lo_half if half == 0 else jnp.logical_not(lo_half)
        qh = jnp.where(keep, q2, jnp.zeros_like(q2))
        k2 = keys(kp_ref, kc_ref, kn_ref, sub, cs)
        return lax.dot_general(qh, k2, (((1,), (1,)), ((), ())), preferred_element_type=F32)

    rowmasks = []
    for sub in range(n_sub):
        qbase = NA_ROWS * i + NA_SUB * sub
        kbase = NA_ROWS * i - NA_SUB if sub == 0 else NA_ROWS * i
        qrow = qbase + lax.broadcasted_iota(jnp.int32, (nq, nk), 0) // GRID_W
        krow = kbase + lax.broadcasted_iota(jnp.int32, (nq, nk), 1) // GRID_W
        rs = jnp.clip(qrow - WIN_R // 2, s0, s1 - WIN_R)
        off = (krow - rs).astype(jnp.uint32)
        rowmasks.append(jnp.where(off < WIN_R, 0.0, NEG_INF).astype(F32))

    s_next = scores(tasks[0])
    outs = []
    for n, (sub, hp, half) in enumerate(tasks):
        s = s_next
        if n + 1 < len(tasks):
            s_next = scores(tasks[n + 1])
        cs = slice(hp * 2 * NA_HEAD_DIM, (hp + 1) * 2 * NA_HEAD_DIM)
        es, dens = [], []
        for a in range(NA_SUB):
            bias = _na_bias_row(t_ref, 2 * hp + half, a)
            for r0 in range(0, GRID_W, NA_SM_ROWS):
                rows = slice(a * GRID_W + r0, a * GRID_W + r0 + NA_SM_ROWS)
                sa = s[rows] + bias[r0:r0 + NA_SM_ROWS] + rowmasks[sub][rows]
                mx = jnp.max(sa, axis=-1, keepdims=True)
                ea = jnp.exp2(sa - mx)
                dens.append(jnp.sum(ea, axis=-1, keepdims=True))
                es.append(ea.astype(BF16))
        e = jnp.concatenate(es, axis=0)
        den = jnp.concatenate(dens, axis=0)
        v2 = keys(vp_ref, vc_ref, vn_ref, sub, cs)
        o = jnp.dot(e, v2, preferred_element_type=F32)
        outs.append(o * (1.0 / den))
        if half == 1:
            o_ref[sub * nq:(sub + 1) * nq, cs] = jnp.where(lo_half, outs[0], outs[1]).astype(BF16)
            outs = []


def _na(p16, t_rel, s0, s1, layer):
    m = p16.shape[0]
    blk = NA_ROWS * GRID_W
    sub = NA_SUB * GRID_W
    nblk = m // blk
    nsub = m // sub
    r = NA_ROWS // NA_SUB

    def cur(col):
        return pl.BlockSpec((blk, NA_DIM), lambda i, a, b: (i, col))

    def prev(col):
        return pl.BlockSpec((sub, NA_DIM), lambda i, a, b: (jnp.maximum(r * i - 1, 0), col))

    def nxt(col):
        return pl.BlockSpec((sub, NA_DIM), lambda i, a, b: (jnp.minimum(r * i + r, nsub - 1), col))

    grid_spec = pltpu.PrefetchScalarGridSpec(
        num_scalar_prefetch=2,
        grid=(nblk,),
        in_specs=[
            cur(COL768["nq"]),
            prev(COL768["nk"]), cur(COL768["nk"]), nxt(COL768["nk"]),
            prev(COL768["nv"]), cur(COL768["nv"]), nxt(COL768["nv"]),
            pl.BlockSpec((None, NA_HEADS, 2, GRID_W, NA_TABLE_W), lambda i, a, b: (layer, 0, 0, 0, 0),
                         pipeline_mode=pl.Buffered(1)),
        ],
        out_specs=pl.BlockSpec((blk, NA_DIM), lambda i, a, b: (i, 0)),
    )
    return pl.pallas_call(
        _na_kernel,
        grid_spec=grid_spec,
        out_shape=jax.ShapeDtypeStruct((m, NA_DIM), BF16),
        compiler_params=_cparams(("parallel",)),
        name="natten",
    )(s0, s1, p16, p16, p16, p16, p16, p16, p16, t_rel)


def _na_bias_table(rpb):
    assert NA_SUB + WIN_R - 1 + NA_SUB - 1 == 2 * WIN_R - 2
    c = np.arange(GRID_W)
    cs = np.clip(c - WIN_C // 2, 0, GRID_W - WIN_C)
    kc = np.arange(GRID_W)[None, :]
    col_ok = (kc >= cs[:, None]) & (kc < cs[:, None] + WIN_C)
    dc = np.clip(kc - c[:, None] + WIN_C - 1, 0, 2 * WIN_C - 2)
    small = jnp.where(jnp.asarray(col_ok), rpb[..., jnp.asarray(dc)].astype(F32) * LOG2E,
                      NEG_INF)
    d, h, ndr = rpb.shape[0], rpb.shape[1], rpb.shape[2]
    flat = small.transpose(0, 1, 3, 2, 4).reshape(d, h, GRID_W, ndr * GRID_W)
    shifted = jnp.pad(flat[..., GRID_W:], ((0, 0), (0, 0), (0, 0), (0, GRID_W)))
    return jnp.stack([flat, shifted], axis=2)


def _hg_mm_nt(x, y):
    return lax.dot_general(x.astype(BF16), y.astype(BF16), (((1,), (1,)), ((), ())),
                           preferred_element_type=F32)


def _hg_prep(q_ref, z, v_ref, lb, hs, code, fwd):
    f = lb + (1.0 - lb) * _sigmoid(z)
    fc = jnp.maximum(f, F_MIN)
    k = 1.0 - f
    qv = q_ref[:, hs].astype(F32)
    q = qv * _sigmoid(qv)
    ones = jnp.ones_like(fc)
    pfx, sfx = (fc, ones) if fwd else (ones, fc)
    a = jnp.where(code == -1, _hg_mm_nt(q, k), 0.0)
    return dict(q=q, k=k, v=v_ref[:, hs], pfx=pfx, sfx=sfx, tot=fc, a=a, code=code, fwd=fwd, hs=hs)


def _hg_level(u, lvl, rowi):
    t = HG_T
    half = 1 << lvl
    fwd, q, k, a, code = u["fwd"], u["q"], u["k"], u["a"], u["code"]
    pfx, sfx, tot = u["pfx"], u["sfx"], u["tot"]
    qm, km = (pfx, sfx) if fwd else (sfx, pfx)
    if half < 8:
        a = jnp.where(code == lvl, _hg_mm_nt(q * qm, k * km), a)
        upper = (rowi & half) != 0
        t3 = tot.reshape(t // 8, 8, HG_DK)
        sib = pltpu.roll(t3, half, 1).reshape(t, HG_DK)
        if half != 4:
            sib = jnp.where(upper, sib, pltpu.roll(t3, 8 - half, 1).reshape(t, HG_DK))
        pfx = pfx * jnp.where(upper, sib, 1.0)
        sfx = sfx * jnp.where(upper, 1.0, sib)
        tot = tot * sib
    else:
        nb = t // (2 * half)
        qside = 1 if fwd else 0

        def sp(x):
            return x.reshape(nb, 2, half, x.shape[-1])

        def jn(lo, hi):
            return jnp.concatenate([lo[:, None], hi[:, None]], axis=1).reshape(t, lo.shape[-1])

        qrows = (sp(q)[:, qside] * sp(qm)[:, qside]).reshape(t // 2, HG_DK)
        blk = _hg_mm_nt(qrows, k * km).reshape(nb, half, t)
        a4 = sp(a)
        sel = jnp.where(sp(code)[:, qside] == lvl, blk, a4[:, qside])
        a = jn(a4[:, 0], sel) if fwd else jn(sel, a4[:, 1])
        t4, p4, s4 = sp(tot), sp(pfx), sp(sfx)
        pfx = jn(p4[:, 0], p4[:, 1] * t4[:, 0])
        sfx = jn(s4[:, 0] * t4[:, 1], s4[:, 1])
        tt = t4[:, 0] * t4[:, 1]
        tot = jn(tt, tt)
    u.update(a=a, pfx=pfx, sfx=sfx, tot=tot)


def _hg_finish_levels(u):
    qm, km = (u["pfx"], u["sfx"]) if u["fwd"] else (u["sfx"], u["pfx"])
    return dict(qd=(u["q"] * qm).astype(BF16), kd=(u["k"] * km).astype(BF16), a=u["a"].astype(BF16),
                v=u["v"], tot=u["tot"][0:1, :], hs=u["hs"])


def _hg_tail(u, o_ref, s_ref, sidx):
    st = s_ref[sidx]
    inter = lax.dot_general(u["qd"], st.astype(BF16), (((1,), (1,)), ((), ())),
                            preferred_element_type=F32)
    intra = jnp.dot(u["a"], u["v"], preferred_element_type=F32)
    o_ref[:, u["hs"]] = inter + intra
    upd = lax.dot_general(u["v"], u["kd"], (((0,), (0,)), ((), ())), preferred_element_type=F32)
    s_ref[sidx] = st * u["tot"] + upd


def _hg_kernel(rf_ref, rb_ref, qf_ref, hf_ref, hfn_ref, vf_ref, qb_ref, hb_ref, hbn_ref, vb_ref,
               wf_ref, wb_ref, lb_ref, of_ref, ob_ref, s_ref, z0_ref):
    c = pl.program_id(0)

    @pl.when(rf_ref[c] == 1)
    def _():
        s_ref[0:HG_HEADS] = jnp.zeros((HG_HEADS, HG_DK, HG_DK), F32)

    @pl.when(rb_ref[c] == 1)
    def _():
        s_ref[HG_HEADS:2 * HG_HEADS] = jnp.zeros((HG_HEADS, HG_DK, HG_DK), F32)

    t = HG_T
    row = lax.broadcasted_iota(jnp.int32, (t, t), 0)
    col = lax.broadcasted_iota(jnp.int32, (t, t), 1)
    x = row ^ col
    hb = (pltpu.bitcast(x.astype(F32), jnp.int32) >> 23) - 127
    diag = jnp.where(row == col, -1, -2)
    code_f = jnp.where(row > col, hb, diag)
    code_b = jnp.where(row < col, hb, diag)
    rowi = lax.broadcasted_iota(jnp.int32, (t, HG_DK), 0)
    n_lvl = HG_T.bit_length() - 1

    def project(hf, hb, hp):
        ps = slice(2 * hp * HG_DK, (2 * hp + 2) * HG_DK)
        return (jnp.dot(hf[...], wf_ref[:, ps], preferred_element_type=F32),
                jnp.dot(hb[...], wb_ref[:, ps], preferred_element_type=F32))

    @pl.when(c == 0)
    def _():
        zf0, zb0 = project(hf_ref, hb_ref, 0)
        z0_ref[0] = zf0
        z0_ref[1] = zb0

    pending = []
    z = {0: (z0_ref[0], z0_ref[1])}
    order = [(h, d) for h in range(HG_HEADS) for d in range(2)]
    for g0 in range(0, len(order), HG_GROUP_UNITS):
        units = []
        for h, d in order[g0:g0 + HG_GROUP_UNITS]:
            hp, i = divmod(h, 2)
            hs = slice(h * HG_DK, (h + 1) * HG_DK)
            zs = slice(i * HG_DK, (i + 1) * HG_DK)
            if d == 0:
                units.append((_hg_prep(qf_ref, z[hp][0][:, zs], vf_ref, lb_ref[0:1, hs], hs, code_f,
                                       True), of_ref, h))
            else:
                units.append((_hg_prep(qb_ref, z[hp][1][:, zs], vb_ref, lb_ref[1:2, hs], hs, code_b,
                                       False), ob_ref, HG_HEADS + h))
        if g0 + HG_GROUP_UNITS < len(order):
            nxt = order[g0 + HG_GROUP_UNITS][0] // 2
            if nxt not in z:
                z[nxt] = project(hf_ref, hb_ref, nxt)
        else:
            zf0, zb0 = project(hfn_ref, hbn_ref, 0)
            z0_ref[0] = zf0
            z0_ref[1] = zb0
        for lvl in range(n_lvl):
            for u, _, _ in units:
                _hg_level(u, lvl, rowi)
        done = [(_hg_finish_levels(u), o_ref, sidx) for u, o_ref, sidx in units]
        for u, o_ref, sidx in pending:
            _hg_tail(u, o_ref, s_ref, sidx)
        pending = done
    for u, o_ref, sidx in pending:
        _hg_tail(u, o_ref, s_ref, sidx)


def _hgrn(p16, hn, w_in, lb, reset_f, reset_b, layer):
    m = p16.shape[0]
    n = m // HG_T
    wcol = FGATE_COL0 // HG_DIM

    def fspec(col):
        return pl.BlockSpec((HG_T, HG_DIM), lambda c, a, b: (c, col))

    def bspec(col):
        return pl.BlockSpec((HG_T, HG_DIM), lambda c, a, b: (n - 1 - c, col))

    def wspec(col):
        return pl.BlockSpec((None, D_MODEL, HG_DIM), lambda c, a, b: (layer, 0, col),
                            pipeline_mode=pl.Buffered(1))

    grid_spec = pltpu.PrefetchScalarGridSpec(
        num_scalar_prefetch=2,
        grid=(n,),
        in_specs=[
            fspec(COL768["cq"]),
            pl.BlockSpec((HG_T, D_MODEL), lambda c, a, b: (c, 0)),
            pl.BlockSpec((HG_T, D_MODEL), lambda c, a, b: (jnp.minimum(c + 1, n - 1), 0)),
            fspec(COL768["ci"]),
            bspec(COL768["cq"]),
            pl.BlockSpec((HG_T, D_MODEL), lambda c, a, b: (n - 1 - c, 0)),
            pl.BlockSpec((HG_T, D_MODEL), lambda c, a, b: (jnp.maximum(n - 2 - c, 0), 0)),
            bspec(COL768["ci"]),
            wspec(wcol), wspec(wcol + 1),
            pl.BlockSpec((None, 2, HG_DIM), lambda c, a, b: (layer, 0, 0)),
        ],
        out_specs=[
            pl.BlockSpec((HG_T, HG_DIM), lambda c, a, b: (c, 0)),
            pl.BlockSpec((HG_T, HG_DIM), lambda c, a, b: (n - 1 - c, 0)),
        ],
        scratch_shapes=[pltpu.VMEM((2 * HG_HEADS, HG_DK, HG_DK), F32),
                        pltpu.VMEM((2, HG_T, 2 * HG_DK), F32)],
    )
    return pl.pallas_call(
        _hg_kernel,
        grid_spec=grid_spec,
        out_shape=[jax.ShapeDtypeStruct((m, HG_DIM), F32), jax.ShapeDtypeStruct((m, HG_DIM), F32)],
        compiler_params=_cparams(("arbitrary",)),
        name="hgrn2",
    )(reset_f, reset_b, p16, hn, hn, p16, p16, hn, hn, p16, w_in, w_in, lb)


def _mix_kernel(st_ref, en_ref, x_ref, ah_ref, ab_ref, ac_ref, ahp_ref, acp_ref, ahn_ref, acn_ref,
                cw_ref, yb_ref, of_ref, ob_ref, cg_ref, ng_ref, ga_ref, gb_ref, gc_ref,
                wa_ref, wb_ref, wc_ref, wo_ref, o_ref):
    i = pl.program_id(0)
    tm = x_ref.shape[0]

    def gated(g_ref, y, w_ref):
        d = jnp.dot(y, w_ref[...], preferred_element_type=F32)
        return (1.0 + jnp.tanh(0.5 * g_ref[...].astype(F32))) * d

    u = ac_ref[...].astype(F32) * ah_ref[...].astype(F32)
    keep_p = jnp.where(st_ref[i] == 1, 0.0, 1.0)
    keep_n = jnp.where(en_ref[i] == 1, 0.0, 1.0)
    h = CONV_HALO
    up_edge = acp_ref[h - 1:h, :].astype(F32) * ahp_ref[h - 1:h, :].astype(F32) * keep_p
    un_edge = acn_ref[0:1, :].astype(F32) * ahn_ref[0:1, :].astype(F32) * keep_n
    rowi = lax.broadcasted_iota(jnp.int32, u.shape, 0)
    u_prev = jnp.where(rowi == 0, up_edge, pltpu.roll(u, 1, 0))
    u_next = jnp.where(rowi == tm - 1, un_edge, pltpu.roll(u, tm - 1, 0))
    cw = cw_ref[...]
    y_a = ab_ref[...].astype(F32) * (u_prev * cw[0:1] + u * cw[1:2] + u_next * cw[2:3])
    mix = gated(ga_ref, y_a.astype(BF16), wa_ref) + gated(gb_ref, yb_ref[...], wb_ref)
    cg = cg_ref[...].astype(F32)
    gate = cg * (0.5 + 0.5 * jnp.tanh(0.5 * cg))
    ng = ng_ref[...]
    parts = []
    for hd in range(HG_HEADS):
        hs = slice(hd * HG_DK, (hd + 1) * HG_DK)
        o = of_ref[:, hs] + ob_ref[:, hs]
        o = o * lax.rsqrt(jnp.mean(o * o, axis=-1, keepdims=True) + EPS)
        parts.append(o * ng[:, hs] * gate[:, hs])
    y_c = jnp.concatenate(parts, axis=-1)
    mix = (mix + gated(gc_ref, y_c.astype(BF16), wc_ref)) * 0.5
    o_ref[...] = x_ref[...] + jnp.dot(mix.astype(BF16), wo_ref[...], preferred_element_type=F32)


def _mix(x, p16, y_b, o_f, o_b, conv_w, ng, wa, wb, wc, wo, is_start, is_end, layer):
    m = x.shape[0]
    tm = MIX_TM
    hb = tm // CONV_HALO
    nh = m // CONV_HALO

    def tok(width, col):
        return pl.BlockSpec((tm, width), lambda i, a, b: (i, col))

    def halo_prev(col):
        return pl.BlockSpec((CONV_HALO, CONV_DIM), lambda i, a, b: (jnp.maximum(i * hb - 1, 0), col))

    def halo_next(col):
        return pl.BlockSpec((CONV_HALO, CONV_DIM),
                            lambda i, a, b: (jnp.minimum((i + 1) * hb, nh - 1), col))

    def weight(rows):
        return pl.BlockSpec((None, rows, D_MODEL), lambda i, a, b: (layer, 0, 0),
                            pipeline_mode=pl.Buffered(1))

    grid_spec = pltpu.PrefetchScalarGridSpec(
        num_scalar_prefetch=2,
        grid=(m // tm,),
        in_specs=[
            tok(D_MODEL, 0),
            tok(CONV_DIM, COL512["a_h"]), tok(CONV_DIM, COL512["a_b"]), tok(CONV_DIM, COL512["a_c"]),
            halo_prev(COL512["a_h"]), halo_prev(COL512["a_c"]),
            halo_next(COL512["a_h"]), halo_next(COL512["a_c"]),
            pl.BlockSpec((None, 3, CONV_DIM), lambda i, a, b: (layer, 0, 0)),
            tok(NA_DIM, 0), tok(HG_DIM, 0), tok(HG_DIM, 0), tok(HG_DIM, COL768["cg"]),
            pl.BlockSpec((None, 1, HG_DIM), lambda i, a, b: (layer, 0, 0)),
            tok(D_MODEL, COL2048["ga"]), tok(D_MODEL, COL2048["gb"]), tok(D_MODEL, COL2048["gc"]),
            weight(CONV_DIM), weight(NA_DIM), weight(HG_DIM), weight(D_MODEL),
        ],
        out_specs=pl.BlockSpec((tm, D_MODEL), lambda i, a, b: (i, 0)),
    )
    return pl.pallas_call(
        _mix_kernel,
        grid_spec=grid_spec,
        out_shape=jax.ShapeDtypeStruct((m, D_MODEL), F32),
        compiler_params=_cparams(("parallel",)),
        name="mix",
    )(is_start, is_end, x, p16, p16, p16, p16, p16, p16, p16, conv_w, y_b, o_f, o_b, p16, ng,
      p16, p16, p16, wa, wb, wc, wo)


def _ffn_kernel(x_ref, g_ref, wg_ref, wu_ref, wd_ref, fg_ref, o_ref, h_ref, *, final):
    j = pl.program_id(1)

    @pl.when(j == 0)
    def _():
        x = x_ref[...]
        h_ref[...] = _rms_scale(x, g_ref[...]).astype(BF16)
        o_ref[...] = x

    h = h_ref[...]
    g = jnp.dot(h, wg_ref[...], preferred_element_type=F32)
    u = jnp.dot(h, wu_ref[...], preferred_element_type=F32)
    a = (g * _sigmoid(g) * u).astype(BF16)
    o_ref[...] += jnp.dot(a, wd_ref[...], preferred_element_type=F32)

    if final:
        @pl.when(j == pl.num_programs(1) - 1)
        def _():
            o_ref[...] = _rms_scale(o_ref[...], fg_ref[...])


def _ffn(x, g, wg, wu, wd, fg, layer, tok_start, tok_len, final):
    tm = next(c for c in FFN_TM_CHOICES if tok_start % c == 0 and tok_len % c == 0)
    off = tok_start // tm
    return pl.pallas_call(
        functools.partial(_ffn_kernel, final=final),
        grid=(tok_len // tm, D_FF // FFN_TF),
        in_specs=[
            pl.BlockSpec((tm, D_MODEL), lambda i, j: (i + off, 0)),
            pl.BlockSpec((None, 1, D_MODEL), lambda i, j: (layer, 0, 0)),
            pl.BlockSpec((None, D_MODEL, FFN_TF), lambda i, j: (layer, 0, j)),
            pl.BlockSpec((None, D_MODEL, FFN_TF), lambda i, j: (layer, 0, j)),
            pl.BlockSpec((None, FFN_TF, D_MODEL), lambda i, j: (layer, j, 0)),
            pl.BlockSpec((1, D_MODEL), lambda i, j: (0, 0)),
        ],
        out_specs=pl.BlockSpec((tm, D_MODEL), lambda i, j: (i, 0)),
        out_shape=jax.ShapeDtypeStruct((tok_len, D_MODEL), F32),
        scratch_shapes=[pltpu.VMEM((tm, D_MODEL), BF16)],
        compiler_params=_cparams(("parallel", "arbitrary")),
        name="ffn_final" if final else "ffn",
    )(x, g, wg, wu, wd, fg)


def _descriptors(seqs, m):
    na_blk = NA_ROWS * GRID_W
    s0 = np.zeros(m // na_blk, np.int32)
    s1 = np.zeros(m // na_blk, np.int32)
    reset_f = np.zeros(m // HG_T, np.int32)
    reset_b = np.zeros(m // HG_T, np.int32)
    is_start = np.zeros(m // MIX_TM, np.int32)
    is_end = np.zeros(m // MIX_TM, np.int32)
    n_chunks = m // HG_T
    for start, length in seqs:
        assert start % na_blk == 0 and length % na_blk == 0 and length // GRID_W >= 2 * WIN_R
        assert start % MIX_TM == 0 and length % MIX_TM == 0
        end = start + length
        s0[start // na_blk:end // na_blk] = start // GRID_W
        s1[start // na_blk:end // na_blk] = end // GRID_W
        reset_f[start // HG_T] = 1
        reset_b[n_chunks - 1 - (end // HG_T - 1)] = 1
        is_start[start // MIX_TM] = 1
        is_end[end // MIX_TM - 1] = 1
    return tuple(jnp.asarray(a) for a in (s0, s1, reset_f, reset_b, is_start, is_end))


def _trunk(x, seqs, group_tokens, norm1_g, w_in, conv_w, t_rel, lb_all, hg_norm_g, w_br_conv, w_br_attn,
           w_br_hgrn, w_mix_out, norm2_g, w_ffn_gate, w_ffn_up, w_ffn_down, final_g):
    m = x.shape[0]
    depth = w_in.shape[0]
    s0, s1, reset_f, reset_b, is_start, is_end = _descriptors(seqs, m)
    ffn_w = (norm2_g, w_ffn_gate, w_ffn_up, w_ffn_down, final_g)
    for l in range(depth):
        p16, hn = _inproj(x, norm1_g, w_in, l)
        y_b = _na(p16, t_rel, s0, s1, l)
        o_f, o_b = _hgrn(p16, hn, w_in, lb_all, reset_f, reset_b, l)
        x = _mix(x, p16, y_b, o_f, o_b, conv_w, hg_norm_g, w_br_conv, w_br_attn, w_br_hgrn,
                 w_mix_out, is_start, is_end, l)
        if l < depth - 1:
            x = _ffn(x, *ffn_w, l, 0, m, False)
    outs = []
    tok = 0
    for n in group_tokens:
        outs.append(_ffn(x, *ffn_w, depth - 1, tok, n, True))
        tok += n
    return outs


def _prepare_params(norm1_g, w_in, conv_w, rpb, hg_lower, hg_norm_g, w_br_conv, w_br_attn, w_br_hgrn,
                    w_mix_out, norm2_g, w_ffn_gate, w_ffn_up, w_ffn_down, final_g):
    sm = jax.nn.softmax(hg_lower.astype(F32), axis=0)
    lb_all = jnp.cumsum(sm, axis=0) - sm[0]
    bf = lambda w: w.astype(BF16)
    return (norm1_g[:, None, :].astype(F32), bf(w_in), conv_w.astype(F32), _na_bias_table(rpb), lb_all,
            hg_norm_g[:, None, :].astype(F32), bf(w_br_conv), bf(w_br_attn), bf(w_br_hgrn),
            bf(w_mix_out), norm2_g[:, None, :].astype(F32), bf(w_ffn_gate), bf(w_ffn_up),
            bf(w_ffn_down), final_g[None, :].astype(F32))


def kernel(x_prompt, x_sample, norm1_g, w_in, conv_w, rpb, hg_lower, hg_norm_g, w_br_conv, w_br_attn,
           w_br_hgrn, w_mix_out, norm2_g, w_ffn_gate, w_ffn_up, w_ffn_down, final_g):
    groups = [x_prompt, x_sample]
    seqs = []
    tok = 0
    for g in groups:
        b, length, _ = g.shape
        for _ in range(b):
            seqs.append((tok, length))
            tok += length
    x = jnp.concatenate([g.reshape(-1, D_MODEL) for g in groups], axis=0)
    params = _prepare_params(norm1_g, w_in, conv_w, rpb, hg_lower, hg_norm_g, w_br_conv, w_br_attn,
                             w_br_hgrn, w_mix_out, norm2_g, w_ffn_gate, w_ffn_up, w_ffn_down, final_g)
    outs = _trunk(x, seqs, [g.shape[0] * g.shape[1] for g in groups], *params)
    return tuple(o.reshape(g.shape) for o, g in zip(outs, groups))
```

```python
import functools

import numpy as np
import jax
import jax.numpy as jnp
from jax import lax
from jax.experimental import pallas as pl
from jax.experimental.pallas import tpu as pltpu

F32 = jnp.float32
BF16 = jnp.bfloat16

D_MODEL = 2048
CONV_DIM = 512
NA_HEADS = 12
NA_HEAD_DIM = 64
NA_DIM = NA_HEADS * NA_HEAD_DIM
GRID_W = 64
WIN_R = 8
WIN_C = 16
HG_HEADS = 6
HG_DK = 128
HG_DIM = HG_HEADS * HG_DK
F_MIN = 1e-30
D_FF = 5632
EPS = 1e-6
NEG_INF = -1e30
LOG2E = 1.4426950408889634

P32_WIDTH = 2 * HG_DIM
P16_WIDTH = 3 * CONV_DIM + 3 * NA_DIM + 3 * HG_DIM + 3 * D_MODEL
FGATE_COL0 = 3 * CONV_DIM + 3 * NA_DIM + HG_DIM
COL768 = dict(nq=2, nk=3, nv=4, cq=5, ci=6, cg=7)
COL512 = dict(a_h=0, a_b=1, a_c=2)
COL2048 = dict(ga=3, gb=4, gc=5)

VMEM_LIMIT_BYTES = 56 * 1024 * 1024

IN_TM, IN_TN = 1024, 1536
FFN_TM_CHOICES, FFN_TF = (1024, 512), 512
MIX_TM = 256
NA_ROWS = 8
NA_HALO = WIN_R // 2
NA_SUB = 4
NA_TABLE_W = (2 * WIN_R - 1) * GRID_W
NA_SM_ROWS = 32
HG_T = 128
HG_GROUP_UNITS = 1
CONV_HALO = 16


def _cparams(sem):
    return pltpu.CompilerParams(dimension_semantics=sem, vmem_limit_bytes=VMEM_LIMIT_BYTES)


def _rms_scale(x, g):
    ms = jnp.mean(x * x, axis=-1, keepdims=True)
    return x * lax.rsqrt(ms + EPS) * g


def _sigmoid(x):
    return 1.0 / (1.0 + jnp.exp(-x))


def _inproj_kernel(x_ref, g_ref, w_ref, o_ref, h_ref):
    @pl.when(pl.program_id(1) == 0)
    def _():
        h_ref[...] = _rms_scale(x_ref[...], g_ref[...]).astype(BF16)

    o_ref[...] = jnp.dot(h_ref[...], w_ref[...], preferred_element_type=F32).astype(BF16)


def _inproj(x, g, w, layer):
    m = x.shape[0]
    nj = P16_WIDTH // IN_TN
    skip = FGATE_COL0 // IN_TN
    nskip = P32_WIDTH // IN_TN
    return pl.pallas_call(
        _inproj_kernel,
        grid=(m // IN_TM, nj),
        in_specs=[
            pl.BlockSpec((IN_TM, D_MODEL), lambda i, j: (i, 0)),
            pl.BlockSpec((None, 1, D_MODEL), lambda i, j: (layer, 0, 0)),
            pl.BlockSpec((None, D_MODEL, IN_TN),
                         lambda i, j: (layer, 0, jnp.where(j >= skip, j + nskip, j))),
        ],
        out_specs=[
            pl.BlockSpec((IN_TM, IN_TN), lambda i, j: (i, j)),
            pl.BlockSpec((IN_TM, D_MODEL), lambda i, j: (i, 0)),
        ],
        out_shape=[jax.ShapeDtypeStruct((m, P16_WIDTH), BF16),
                   jax.ShapeDtypeStruct((m, D_MODEL), BF16)],
        compiler_params=_cparams(("parallel", "arbitrary")),
        name="inproj",
    )(x, g, w)


def _na_block(i, s0, s1, q_ref, k_refs, v_refs, t_ref, o_ref, nsub_rows, clamped):
    nq = nsub_rows * GRID_W
    nkr = nsub_rows + WIN_R
    nk = nkr * GRID_W
    lane = lax.broadcasted_iota(jnp.int32, (nq, 2 * NA_HEAD_DIM), 1)
    lo_half = lane < NA_HEAD_DIM
    n_sub = NA_ROWS // nsub_rows
    tasks = [(sub, hp, half) for sub in range(n_sub) for hp in range(NA_HEADS // 2) for half in range(2)]

    def keys(refs, sub, cs):
        r0 = nsub_rows * sub - NA_HALO
        r1 = r0 + nkr
        parts = []
        if r0 < 0:
            parts.append(refs[0][(r0 + NA_HALO) * GRID_W:(min(r1, 0) + NA_HALO) * GRID_W, cs])
        parts.append(refs[1][max(r0, 0) * GRID_W:min(r1, NA_ROWS) * GRID_W, cs])
        if r1 > NA_ROWS:
            parts.append(refs[2][0:(r1 - NA_ROWS) * GRID_W, cs])
        return jnp.concatenate(parts, axis=0)

    def bias_row(head, a):
        shift = NA_HALO - 1 - a
        off = (shift // 2) * 2 * GRID_W
        return t_ref[head, shift % 2, :, off:off + nk]

    def scores(task):
        sub, hp, half = task
        cs = slice(hp * 2 * NA_HEAD_DIM, (hp + 1) * 2 * NA_HEAD_DIM)
        q2 = q_ref[sub * nq:(sub + 1) * nq, cs] * (NA_HEAD_DIM ** -0.5 * LOG2E)
        keep = lo_half if half == 0 else jnp.logical_not(lo_half)
        qh = jnp.where(keep, q2, jnp.zeros_like(q2))
        return lax.dot_general(qh, keys(k_refs, sub, cs), (((1,), (1,)), ((), ())),
                               preferred_element_type=F32)

    rowmasks = []
    if clamped:
        for sub in range(n_sub):
            qbase = NA_ROWS * i + nsub_rows * sub
            qrow = qbase + lax.broadcasted_iota(jnp.int32, (nq, nk), 0) // GRID_W
            krow = qbase - NA_HALO + lax.broadcasted_iota(jnp.int32, (nq, nk), 1) // GRID_W
            rs = jnp.clip(qrow - WIN_R // 2, s0, s1 - WIN_R)
            off = (krow - rs).astype(jnp.uint32)
            rowmasks.append(jnp.where(off < WIN_R, 0.0, NEG_INF).astype(F32))

    s_next = scores(tasks[0])
    outs = []
    for n, (sub, hp, half) in enumerate(tasks):
        s = s_next
        if n + 1 < len(tasks):
            s_next = scores(tasks[n + 1])
        cs = slice(hp * 2 * NA_HEAD_DIM, (hp + 1) * 2 * NA_HEAD_DIM)
        es, dens = [], []
        for a in range(nsub_rows):
            bias = bias_row(2 * hp + half, a)
            for r0 in range(0, GRID_W, NA_SM_ROWS):
                rows = slice(a * GRID_W + r0, a * GRID_W + r0 + NA_SM_ROWS)
                sa = s[rows] + bias[r0:r0 + NA_SM_ROWS]
                if clamped:
                    sa = sa + rowmasks[sub][rows]
                mx = jnp.max(sa, axis=-1, keepdims=True)
                ea = jnp.exp2(sa - mx)
                dens.append(jnp.sum(ea, axis=-1, keepdims=True))
                es.append(ea.astype(BF16))
        e = jnp.concatenate(es, axis=0)
        den = jnp.concatenate(dens, axis=0)
        o = jnp.dot(e, keys(v_refs, sub, cs), preferred_element_type=F32)
        outs.append(o * (1.0 / den))
        if half == 1:
            o_ref[sub * nq:(sub + 1) * nq, cs] = jnp.where(lo_half, outs[0], outs[1]).astype(BF16)
            outs = []


def _na_kernel(s0_ref, s1_ref, q_ref, kp_ref, kc_ref, kn_ref, vp_ref, vc_ref, vn_ref, te_ref, ti_ref,
               o_ref):
    i = pl.program_id(0)
    s0 = s0_ref[i]
    s1 = s1_ref[i]
    k_refs = (kp_ref, kc_ref, kn_ref)
    v_refs = (vp_ref, vc_ref, vn_ref)
    at_end = jnp.logical_or(NA_ROWS * i == s0, NA_ROWS * (i + 1) == s1)

    @pl.when(at_end)
    def _():
        _na_block(i, s0, s1, q_ref, k_refs, v_refs, te_ref, o_ref, NA_HALO, True)

    @pl.when(jnp.logical_not(at_end))
    def _():
        _na_block(i, s0, s1, q_ref, k_refs, v_refs, ti_ref, o_ref, NA_SUB, False)


def _na(p16, t_rel, s0, s1, layer):
    t_edge, t_interior = t_rel
    m = p16.shape[0]
    blk = NA_ROWS * GRID_W
    sub = NA_HALO * GRID_W
    nblk = m // blk
    nsub = m // sub
    r = NA_ROWS // NA_HALO

    def cur(col):
        return pl.BlockSpec((blk, NA_DIM), lambda i, a, b: (i, col))

    def prev(col):
        return pl.BlockSpec((sub, NA_DIM), lambda i, a, b: (jnp.maximum(r * i - 1, 0), col))

    def nxt(col):
        return pl.BlockSpec((sub, NA_DIM), lambda i, a, b: (jnp.minimum(r * i + r, nsub - 1), col))

    table = pl.BlockSpec((None, NA_HEADS, 2, GRID_W, NA_TABLE_W), lambda i, a, b: (layer, 0, 0, 0, 0),
                         pipeline_mode=pl.Buffered(1))
    grid_spec = pltpu.PrefetchScalarGridSpec(
        num_scalar_prefetch=2,
        grid=(nblk,),
        in_specs=[
            cur(COL768["nq"]),
            prev(COL768["nk"]), cur(COL768["nk"]), nxt(COL768["nk"]),
            prev(COL768["nv"]), cur(COL768["nv"]), nxt(COL768["nv"]),
            table, table,
        ],
        out_specs=pl.BlockSpec((blk, NA_DIM), lambda i, a, b: (i, 0)),
    )
    return pl.pallas_call(
        _na_kernel,
        grid_spec=grid_spec,
        out_shape=jax.ShapeDtypeStruct((m, NA_DIM), BF16),
        compiler_params=_cparams(("parallel",)),
        name="natten",
    )(s0, s1, p16, p16, p16, p16, p16, p16, p16, t_edge, t_interior)


def _na_bias_table(rpb):
    assert 2 * NA_HALO + WIN_R - 2 == 2 * WIN_R - 2
    c = np.arange(GRID_W)
    cs = np.clip(c - WIN_C // 2, 0, GRID_W - WIN_C)
    kc = np.arange(GRID_W)[None, :]
    col_ok = (kc >= cs[:, None]) & (kc < cs[:, None] + WIN_C)
    dc = np.clip(kc - c[:, None] + WIN_C - 1, 0, 2 * WIN_C - 2)
    small = jnp.where(jnp.asarray(col_ok), rpb[..., jnp.asarray(dc)].astype(F32) * LOG2E,
                      NEG_INF)
    d, h, ndr = rpb.shape[0], rpb.shape[1], rpb.shape[2]
    dr = np.arange(ndr) - (WIN_R - 1)
    centred = jnp.asarray((dr >= -(WIN_R // 2)) & (dr < WIN_R - WIN_R // 2))[:, None, None]

    def flatten(tbl):
        flat = tbl.transpose(0, 1, 3, 2, 4).reshape(d, h, GRID_W, ndr * GRID_W)
        shifted = jnp.pad(flat[..., GRID_W:], ((0, 0), (0, 0), (0, 0), (0, GRID_W)))
        return jnp.stack([flat, shifted], axis=2)

    return flatten(small), flatten(jnp.where(centred, small, NEG_INF))


def _hg_mm_nt(x, y):
    return lax.dot_general(x.astype(BF16), y.astype(BF16), (((1,), (1,)), ((), ())),
                           preferred_element_type=F32)


def _hg_prep(q_ref, z, v_ref, lb, hs, code, fwd):
    f = lb + (1.0 - lb) * _sigmoid(z)
    fc = jnp.maximum(f, F_MIN)
    k = 1.0 - f
    qv = q_ref[:, hs].astype(F32)
    q = qv * _sigmoid(qv)
    ones = jnp.ones_like(fc)
    pfx, sfx = (fc, ones) if fwd else (ones, fc)
    a = jnp.where(code == -1, _hg_mm_nt(q, k), 0.0)
    return dict(q=q, k=k, v=v_ref[:, hs], pfx=pfx, sfx=sfx, tot=fc, a=a, code=code, fwd=fwd, hs=hs)


def _hg_level(u, lvl, rowi):
    t = HG_T
    half = 1 << lvl
    fwd, q, k, a, code = u["fwd"], u["q"], u["k"], u["a"], u["code"]
    pfx, sfx, tot = u["pfx"], u["sfx"], u["tot"]
    qm, km = (pfx, sfx) if fwd else (sfx, pfx)
    if half < 8:
        a = jnp.where(code == lvl, _hg_mm_nt(q * qm, k * km), a)
        upper = (rowi & half) != 0
        t3 = tot.reshape(t // 8, 8, HG_DK)
        sib = pltpu.roll(t3, half, 1).reshape(t, HG_DK)
        if half != 4:
            sib = jnp.where(upper, sib, pltpu.roll(t3, 8 - half, 1).reshape(t, HG_DK))
        pfx = pfx * jnp.where(upper, sib, 1.0)
        sfx = sfx * jnp.where(upper, 1.0, sib)
        tot = tot * sib
    else:
        nb = t // (2 * half)
        qside = 1 if fwd else 0

        def sp(x):
            return x.reshape(nb, 2, half, x.shape[-1])

        def jn(lo, hi):
            return jnp.concatenate([lo[:, None], hi[:, None]], axis=1).reshape(t, lo.shape[-1])

        qrows = (sp(q)[:, qside] * sp(qm)[:, qside]).reshape(t // 2, HG_DK)
        blk = _hg_mm_nt(qrows, k * km).reshape(nb, half, t)
        a4 = sp(a)
        sel = jnp.where(sp(code)[:, qside] == lvl, blk, a4[:, qside])
        a = jn(a4[:, 0], sel) if fwd else jn(sel, a4[:, 1])
        t4, p4, s4 = sp(tot), sp(pfx), sp(sfx)
        pfx = jn(p4[:, 0], p4[:, 1] * t4[:, 0])
        sfx = jn(s4[:, 0] * t4[:, 1], s4[:, 1])
        tt = t4[:, 0] * t4[:, 1]
        tot = jn(tt, tt)
    u.update(a=a, pfx=pfx, sfx=sfx, tot=tot)


def _hg_finish_levels(u):
    qm, km = (u["pfx"], u["sfx"]) if u["fwd"] else (u["sfx"], u["pfx"])
    return dict(qd=(u["q"] * qm).astype(BF16), kd=(u["k"] * km).astype(BF16), a=u["a"].astype(BF16),
                v=u["v"], tot=u["tot"][0:1, :], hs=u["hs"])


def _hg_tail(u, o_ref, s_ref, sidx):
    st = s_ref[sidx]
    inter = lax.dot_general(u["qd"], st.astype(BF16), (((1,), (1,)), ((), ())),
                            preferred_element_type=F32)
    intra = jnp.dot(u["a"], u["v"], preferred_element_type=F32)
    o_ref[:, u["hs"]] = inter + intra
    upd = lax.dot_general(u["v"], u["kd"], (((0,), (0,)), ((), ())), preferred_element_type=F32)
    s_ref[sidx] = st * u["tot"] + upd


def _hg_kernel(rf_ref, rb_ref, qf_ref, hf_ref, hfn_ref, vf_ref, qb_ref, hb_ref, hbn_ref, vb_ref,
               wf_ref, wb_ref, lb_ref, of_ref, ob_ref, s_ref, z0_ref):
    c = pl.program_id(0)

    @pl.when(rf_ref[c] == 1)
    def _():
        s_ref[0:HG_HEADS] = jnp.zeros((HG_HEADS, HG_DK, HG_DK), F32)

    @pl.when(rb_ref[c] == 1)
    def _():
        s_ref[HG_HEADS:2 * HG_HEADS] = jnp.zeros((HG_HEADS, HG_DK, HG_DK), F32)

    t = HG_T
    row = lax.broadcasted_iota(jnp.int32, (t, t), 0)
    col = lax.broadcasted_iota(jnp.int32, (t, t), 1)
    x = row ^ col
    hb = (pltpu.bitcast(x.astype(F32), jnp.int32) >> 23) - 127
    diag = jnp.where(row == col, -1, -2)
    code_f = jnp.where(row > col, hb, diag)
    code_b = jnp.where(row < col, hb, diag)
    rowi = lax.broadcasted_iota(jnp.int32, (t, HG_DK), 0)
    n_lvl = HG_T.bit_length() - 1

    def project(hf, hb, hp):
        ps = slice(2 * hp * HG_DK, (2 * hp + 2) * HG_DK)
        return (jnp.dot(hf[...], wf_ref[:, ps], preferred_element_type=F32),
                jnp.dot(hb[...], wb_ref[:, ps], preferred_element_type=F32))

    @pl.when(c == 0)
    def _():
        zf0, zb0 = project(hf_ref, hb_ref, 0)
        z0_ref[0] = zf0
        z0_ref[1] = zb0

    pending = []
    z = {0: (z0_ref[0], z0_ref[1])}
    order = [(h, d) for h in range(HG_HEADS) for d in range(2)]
    for g0 in range(0, len(order), HG_GROUP_UNITS):
        units = []
        for h, d in order[g0:g0 + HG_GROUP_UNITS]:
            hp, i = divmod(h, 2)
            hs = slice(h * HG_DK, (h + 1) * HG_DK)
            zs = slice(i * HG_DK, (i + 1) * HG_DK)
            if d == 0:
                units.append((_hg_prep(qf_ref, z[hp][0][:, zs], vf_ref, lb_ref[0:1, hs], hs, code_f,
                                       True), of_ref, h))
            else:
                units.append((_hg_prep(qb_ref, z[hp][1][:, zs], vb_ref, lb_ref[1:2, hs], hs, code_b,
                                       False), ob_ref, HG_HEADS + h))
        if g0 + HG_GROUP_UNITS < len(order):
            nxt = order[g0 + HG_GROUP_UNITS][0] // 2
            if nxt not in z:
                z[nxt] = project(hf_ref, hb_ref, nxt)
        else:
            zf0, zb0 = project(hfn_ref, hbn_ref, 0)
            z0_ref[0] = zf0
            z0_ref[1] = zb0
        for lvl in range(n_lvl):
            for u, _, _ in units:
                _hg_level(u, lvl, rowi)
        done = [(_hg_finish_levels(u), o_ref, sidx) for u, o_ref, sidx in units]
        for u, o_ref, sidx in pending:
            _hg_tail(u, o_ref, s_ref, sidx)
        pending = done
    for u, o_ref, sidx in pending:
        _hg_tail(u, o_ref, s_ref, sidx)


def _hgrn(p16, hn, w_in, lb, reset_f, reset_b, layer):
    m = p16.shape[0]
    n = m // HG_T
    wcol = FGATE_COL0 // HG_DIM

    def fspec(col):
        return pl.BlockSpec((HG_T, HG_DIM), lambda c, a, b: (c, col))

    def bspec(col):
        return pl.BlockSpec((HG_T, HG_DIM), lambda c, a, b: (n - 1 - c, col))

    def wspec(col):
        return pl.BlockSpec((None, D_MODEL, HG_DIM), lambda c, a, b: (layer, 0, col),
                            pipeline_mode=pl.Buffered(1))

    grid_spec = pltpu.PrefetchScalarGridSpec(
        num_scalar_prefetch=2,
        grid=(n,),
        in_specs=[
            fspec(COL768["cq"]),
            pl.BlockSpec((HG_T, D_MODEL), lambda c, a, b: (c, 0)),
            pl.BlockSpec((HG_T, D_MODEL), lambda c, a, b: (jnp.minimum(c + 1, n - 1), 0)),
            fspec(COL768["ci"]),
            bspec(COL768["cq"]),
            pl.BlockSpec((HG_T, D_MODEL), lambda c, a, b: (n - 1 - c, 0)),
            pl.BlockSpec((HG_T, D_MODEL), lambda c, a, b: (jnp.maximum(n - 2 - c, 0), 0)),
            bspec(COL768["ci"]),
            wspec(wcol), wspec(wcol + 1),
            pl.BlockSpec((None, 2, HG_DIM), lambda c, a, b: (layer, 0, 0)),
        ],
        out_specs=[
            pl.BlockSpec((HG_T, HG_DIM), lambda c, a, b: (c, 0)),
            pl.BlockSpec((HG_T, HG_DIM), lambda c, a, b: (n - 1 - c, 0)),
        ],
        scratch_shapes=[pltpu.VMEM((2 * HG_HEADS, HG_DK, HG_DK), F32),
                        pltpu.VMEM((2, HG_T, 2 * HG_DK), F32)],
    )
    return pl.pallas_call(
        _hg_kernel,
        grid_spec=grid_spec,
        out_shape=[jax.ShapeDtypeStruct((m, HG_DIM), F32), jax.ShapeDtypeStruct((m, HG_DIM), F32)],
        compiler_params=_cparams(("arbitrary",)),
        name="hgrn2",
    )(reset_f, reset_b, p16, hn, hn, p16, p16, hn, hn, p16, w_in, w_in, lb)


def _mix_kernel(st_ref, en_ref, x_ref, ah_ref, ab_ref, ac_ref, ahp_ref, acp_ref, ahn_ref, acn_ref,
                cw_ref, yb_ref, of_ref, ob_ref, cg_ref, ng_ref, ga_ref, gb_ref, gc_ref,
                wa_ref, wb_ref, wc_ref, wo_ref, o_ref):
    i = pl.program_id(0)
    tm = x_ref.shape[0]

    def gated(g_ref, y, w_ref):
        return _sigmoid(g_ref[...].astype(F32)) * jnp.dot(y, w_ref[...], preferred_element_type=F32)

    u = ac_ref[...].astype(F32) * ah_ref[...].astype(F32)
    keep_p = jnp.where(st_ref[i] == 1, 0.0, 1.0)
    keep_n = jnp.where(en_ref[i] == 1, 0.0, 1.0)
    h = CONV_HALO
    up_edge = acp_ref[h - 1:h, :].astype(F32) * ahp_ref[h - 1:h, :].astype(F32) * keep_p
    un_edge = acn_ref[0:1, :].astype(F32) * ahn_ref[0:1, :].astype(F32) * keep_n
    rowi = lax.broadcasted_iota(jnp.int32, u.shape, 0)
    u_prev = jnp.where(rowi == 0, up_edge, pltpu.roll(u, 1, 0))
    u_next = jnp.where(rowi == tm - 1, un_edge, pltpu.roll(u, tm - 1, 0))
    cw = cw_ref[...]
    y_a = ab_ref[...].astype(F32) * (u_prev * cw[0:1] + u * cw[1:2] + u_next * cw[2:3])
    mix = gated(ga_ref, y_a.astype(BF16), wa_ref) + gated(gb_ref, yb_ref[...], wb_ref)
    cg = cg_ref[...].astype(F32)
    gate = cg * _sigmoid(cg)
    ng = ng_ref[...]
    parts = []
    for hd in range(HG_HEADS):
        hs = slice(hd * HG_DK, (hd + 1) * HG_DK)
        o = of_ref[:, hs] + ob_ref[:, hs]
        o = o * lax.rsqrt(jnp.mean(o * o, axis=-1, keepdims=True) + EPS)
        parts.append(o * ng[:, hs] * gate[:, hs])
    y_c = jnp.concatenate(parts, axis=-1)
    mix = mix + gated(gc_ref, y_c.astype(BF16), wc_ref)
    o_ref[...] = x_ref[...] + jnp.dot(mix.astype(BF16), wo_ref[...], preferred_element_type=F32)


def _mix(x, p16, y_b, o_f, o_b, conv_w, ng, wa, wb, wc, wo, is_start, is_end, layer):
    m = x.shape[0]
    tm = MIX_TM
    hb = tm // CONV_HALO
    nh = m // CONV_HALO

    def tok(width, col):
        return pl.BlockSpec((tm, width), lambda i, a, b: (i, col))

    def halo_prev(col):
        return pl.BlockSpec((CONV_HALO, CONV_DIM), lambda i, a, b: (jnp.maximum(i * hb - 1, 0), col))

    def halo_next(col):
        return pl.BlockSpec((CONV_HALO, CONV_DIM),
                            lambda i, a, b: (jnp.minimum((i + 1) * hb, nh - 1), col))

    def weight(rows):
        return pl.BlockSpec((None, rows, D_MODEL), lambda i, a, b: (layer, 0, 0),
                            pipeline_mode=pl.Buffered(1))

    grid_spec = pltpu.PrefetchScalarGridSpec(
        num_scalar_prefetch=2,
        grid=(m // tm,),
        in_specs=[
            tok(D_MODEL, 0),
            tok(CONV_DIM, COL512["a_h"]), tok(CONV_DIM, COL512["a_b"]), tok(CONV_DIM, COL512["a_c"]),
            halo_prev(COL512["a_h"]), halo_prev(COL512["a_c"]),
            halo_next(COL512["a_h"]), halo_next(COL512["a_c"]),
            pl.BlockSpec((None, 3, CONV_DIM), lambda i, a, b: (layer, 0, 0)),
            tok(NA_DIM, 0), tok(HG_DIM, 0), tok(HG_DIM, 0), tok(HG_DIM, COL768["cg"]),
            pl.BlockSpec((None, 1, HG_DIM), lambda i, a, b: (layer, 0, 0)),
            tok(D_MODEL, COL2048["ga"]), tok(D_MODEL, COL2048["gb"]), tok(D_MODEL, COL2048["gc"]),
            weight(CONV_DIM), weight(NA_DIM), weight(HG_DIM), weight(D_MODEL),
        ],
        out_specs=pl.BlockSpec((tm, D_MODEL), lambda i, a, b: (i, 0)),
    )
    return pl.pallas_call(
        _mix_kernel,
        grid_spec=grid_spec,
        out_shape=jax.ShapeDtypeStruct((m, D_MODEL), F32),
        compiler_params=_cparams(("parallel",)),
        name="mix",
    )(is_start, is_end, x, p16, p16, p16, p16, p16, p16, p16, conv_w, y_b, o_f, o_b, p16, ng,
      p16, p16, p16, wa, wb, wc, wo)


def _ffn_kernel(x_ref, g_ref, wg_ref, wu_ref, wd_ref, fg_ref, o_ref, h_ref, *, final):
    j = pl.program_id(1)

    @pl.when(j == 0)
    def _():
        x = x_ref[...]
        h_ref[...] = _rms_scale(x, g_ref[...]).astype(BF16)
        o_ref[...] = x

    h = h_ref[...]
    g = jnp.dot(h, wg_ref[...], preferred_element_type=F32)
    u = jnp.dot(h, wu_ref[...], preferred_element_type=F32)
    a = (g * _sigmoid(g) * u).astype(BF16)
    o_ref[...] += jnp.dot(a, wd_ref[...], preferred_element_type=F32)

    if final:
        @pl.when(j == pl.num_programs(1) - 1)
        def _():
            o_ref[...] = _rms_scale(o_ref[...], fg_ref[...])


def _ffn(x, g, wg, wu, wd, fg, layer, tok_start, tok_len, final):
    tm = next(c for c in FFN_TM_CHOICES if tok_start % c == 0 and tok_len % c == 0)
    off = tok_start // tm
    return pl.pallas_call(
        functools.partial(_ffn_kernel, final=final),
        grid=(tok_len // tm, D_FF // FFN_TF),
        in_specs=[
            pl.BlockSpec((tm, D_MODEL), lambda i, j: (i + off, 0)),
            pl.BlockSpec((None, 1, D_MODEL), lambda i, j: (layer, 0, 0)),
            pl.BlockSpec((None, D_MODEL, FFN_TF), lambda i, j: (layer, 0, j)),
            pl.BlockSpec((None, D_MODEL, FFN_TF), lambda i, j: (layer, 0, j)),
            pl.BlockSpec((None, FFN_TF, D_MODEL), lambda i, j: (layer, j, 0)),
            pl.BlockSpec((1, D_MODEL), lambda i, j: (0, 0)),
        ],
        out_specs=pl.BlockSpec((tm, D_MODEL), lambda i, j: (i, 0)),
        out_shape=jax.ShapeDtypeStruct((tok_len, D_MODEL), F32),
        scratch_shapes=[pltpu.VMEM((tm, D_MODEL), BF16)],
        compiler_params=_cparams(("parallel", "arbitrary")),
        name="ffn_final" if final else "ffn",
    )(x, g, wg, wu, wd, fg)


def _descriptors(seqs, m):
    na_blk = NA_ROWS * GRID_W
    s0 = np.zeros(m // na_blk, np.int32)
    s1 = np.zeros(m // na_blk, np.int32)
    reset_f = np.zeros(m // HG_T, np.int32)
    reset_b = np.zeros(m // HG_T, np.int32)
    is_start = np.zeros(m // MIX_TM, np.int32)
    is_end = np.zeros(m // MIX_TM, np.int32)
    n_chunks = m // HG_T
    for start, length in seqs:
        assert start % na_blk == 0 and length % na_blk == 0 and length // GRID_W >= 2 * WIN_R
        assert start % MIX_TM == 0 and length % MIX_TM == 0
        end = start + length
        s0[start // na_blk:end // na_blk] = start // GRID_W
        s1[start // na_blk:end // na_blk] = end // GRID_W
        reset_f[start // HG_T] = 1
        reset_b[n_chunks - 1 - (end // HG_T - 1)] = 1
        is_start[start // MIX_TM] = 1
        is_end[end // MIX_TM - 1] = 1
    return tuple(jnp.asarray(a) for a in (s0, s1, reset_f, reset_b, is_start, is_end))


def _trunk(x, seqs, group_tokens, norm1_g, w_in, conv_w, t_rel, lb_all, hg_norm_g, w_br_conv, w_br_attn,
           w_br_hgrn, w_mix_out, norm2_g, w_ffn_gate, w_ffn_up, w_ffn_down, final_g):
    m = x.shape[0]
    depth = w_in.shape[0]
    s0, s1, reset_f, reset_b, is_start, is_end = _descriptors(seqs, m)
    ffn_w = (norm2_g, w_ffn_gate, w_ffn_up, w_ffn_down, final_g)
    for l in range(depth):
        p16, hn = _inproj(x, norm1_g, w_in, l)
        y_b = _na(p16, t_rel, s0, s1, l)
        o_f, o_b = _hgrn(p16, hn, w_in, lb_all, reset_f, reset_b, l)
        x = _mix(x, p16, y_b, o_f, o_b, conv_w, hg_norm_g, w_br_conv, w_br_attn, w_br_hgrn,
                 w_mix_out, is_start, is_end, l)
        if l < depth - 1:
            x = _ffn(x, *ffn_w, l, 0, m, False)
    outs = []
    tok = 0
    for n in group_tokens:
        outs.append(_ffn(x, *ffn_w, depth - 1, tok, n, True))
        tok += n
    return outs


def _prepare_params(norm1_g, w_in, conv_w, rpb, hg_lower, hg_norm_g, w_br_conv, w_br_attn, w_br_hgrn,
                    w_mix_out, norm2_g, w_ffn_gate, w_ffn_up, w_ffn_down, final_g):
    sm = jax.nn.softmax(hg_lower.astype(F32), axis=0)
    lb_all = jnp.cumsum(sm, axis=0) - sm[0]
    bf = lambda w: w.astype(BF16)
    return (norm1_g[:, None, :].astype(F32), bf(w_in), conv_w.astype(F32), _na_bias_table(rpb), lb_all,
            hg_norm_g[:, None, :].astype(F32), bf(w_br_conv), bf(w_br_attn), bf(w_br_hgrn),
            bf(w_mix_out), norm2_g[:, None, :].astype(F32), bf(w_ffn_gate), bf(w_ffn_up),
            bf(w_ffn_down), final_g[None, :].astype(F32))


def kernel(x_prompt, x_sample, norm1_g, w_in, conv_w, rpb, hg_lower, hg_norm_g, w_br_conv, w_br_attn,
           w_br_hgrn, w_mix_out, norm2_g, w_ffn_gate, w_ffn_up, w_ffn_down, final_g):
    groups = [x_prompt, x_sample]
    seqs = []
    tok = 0
    for g in groups:
        b, length, _ = g.shape
        for _ in range(b):
            seqs.append((tok, length))
            tok += length
    x = jnp.concatenate([g.reshape(-1, D_MODEL) for g in groups], axis=0)
    params = _prepare_params(norm1_g, w_in, conv_w, rpb, hg_lower, hg_norm_g, w_br_conv, w_br_attn,
                             w_br_hgrn, w_mix_out, norm2_g, w_ffn_gate, w_ffn_up, w_ffn_down, final_g)
    outs = _trunk(x, seqs, [g.shape[0] * g.shape[1] for g in groups], *params)
    return tuple(o.reshape(g.shape) for o, g in zip(outs, groups))
```

```python
import functools

import numpy as np
import jax
import jax.numpy as jnp
from jax import lax
from jax.experimental import pallas as pl
from jax.experimental.pallas import tpu as pltpu

F32 = jnp.float32
BF16 = jnp.bfloat16

D_MODEL = 2048
CONV_DIM = 512
NA_HEADS = 12
NA_HEAD_DIM = 64
NA_DIM = NA_HEADS * NA_HEAD_DIM
GRID_W = 64
WIN_R = 8
WIN_C = 16
HG_HEADS = 6
HG_DK = 128
HG_DIM = HG_HEADS * HG_DK
F_MIN = 1e-30
D_FF = 5632
EPS = 1e-6
NEG_INF = -1e30
LOG2E = 1.4426950408889634

P32_WIDTH = 2 * HG_DIM
P16_WIDTH = 3 * CONV_DIM + 3 * NA_DIM + 3 * HG_DIM + 3 * D_MODEL
FGATE_COL0 = 3 * CONV_DIM + 3 * NA_DIM + HG_DIM
COL768 = dict(nq=2, nk=3, nv=4, cq=5, ci=6, cg=7)
COL512 = dict(a_h=0, a_b=1, a_c=2)
COL2048 = dict(ga=3, gb=4, gc=5)

VMEM_LIMIT_BYTES = 56 * 1024 * 1024

IN_TM, IN_TN = 1024, 1536
FFN_TM_CHOICES, FFN_TF = (1024, 512), 512
MIX_TM = 256
NA_ROWS = 8
NA_HALO = WIN_R // 2
NA_SUB = 4
NA_TABLE_W = (2 * WIN_R - 1) * GRID_W
NA_SM_ROWS = 32
HG_T = 128
HG_GROUP_UNITS = 1
CONV_HALO = 16


def _cparams(sem):
    return pltpu.CompilerParams(dimension_semantics=sem, vmem_limit_bytes=VMEM_LIMIT_BYTES)


def _rms_scale(x, g):
    ms = jnp.mean(x * x, axis=-1, keepdims=True)
    return x * lax.rsqrt(ms + EPS) * g


def _sigmoid(x):
    return 1.0 / (1.0 + jnp.exp(-x))


def _inproj_kernel(x_ref, g_ref, w_ref, *refs):
    o_ref, h_ref = refs[-2:]

    @pl.when(pl.program_id(1) == 0)
    def _():
        h_ref[...] = _rms_scale(x_ref[...], g_ref[...]).astype(BF16)

    o_ref[...] = jnp.dot(h_ref[...], w_ref[...], preferred_element_type=F32).astype(BF16)


def _inproj(sources, m, g, w, layer):
    nj = P16_WIDTH // IN_TN
    skip = FGATE_COL0 // IN_TN
    nskip = P32_WIDTH // IN_TN
    outs = ()
    for x, tok0 in sources:
        off = tok0 // IN_TM
        outs = pl.pallas_call(
            _inproj_kernel,
            grid=(x.shape[0] // IN_TM, nj),
            in_specs=[
                pl.BlockSpec((IN_TM, D_MODEL), lambda i, j: (i, 0)),
                pl.BlockSpec((None, 1, D_MODEL), lambda i, j: (layer, 0, 0)),
                pl.BlockSpec((None, D_MODEL, IN_TN),
                             lambda i, j: (layer, 0, jnp.where(j >= skip, j + nskip, j))),
            ] + [pl.BlockSpec(memory_space=pl.ANY)] * len(outs),
            out_specs=[
                pl.BlockSpec((IN_TM, IN_TN), lambda i, j, off=off: (i + off, j)),
                pl.BlockSpec((IN_TM, D_MODEL), lambda i, j, off=off: (i + off, 0)),
            ],
            out_shape=[jax.ShapeDtypeStruct((m, P16_WIDTH), BF16),
                       jax.ShapeDtypeStruct((m, D_MODEL), BF16)],
            input_output_aliases={3 + k: k for k in range(len(outs))},
            compiler_params=_cparams(("parallel", "arbitrary")),
            name="inproj",
        )(x, g, w, *outs)
    return outs


def _na_block(i, s0, s1, q_ref, k_refs, v_refs, t_ref, o_ref, nsub_rows, clamped):
    nq = nsub_rows * GRID_W
    nkr = nsub_rows + WIN_R
    nk = nkr * GRID_W
    lane = lax.broadcasted_iota(jnp.int32, (nq, 2 * NA_HEAD_DIM), 1)
    lo_half = lane < NA_HEAD_DIM
    n_sub = NA_ROWS // nsub_rows
    tasks = [(sub, hp, half) for sub in range(n_sub) for hp in range(NA_HEADS // 2) for half in range(2)]

    def keys(refs, sub, cs):
        r0 = nsub_rows * sub - NA_HALO
        r1 = r0 + nkr
        parts = []
        if r0 < 0:
            parts.append(refs[0][(r0 + NA_HALO) * GRID_W:(min(r1, 0) + NA_HALO) * GRID_W, cs])
        parts.append(refs[1][max(r0, 0) * GRID_W:min(r1, NA_ROWS) * GRID_W, cs])
        if r1 > NA_ROWS:
            parts.append(refs[2][0:(r1 - NA_ROWS) * GRID_W, cs])
        return jnp.concatenate(parts, axis=0)

    def bias_row(head, a):
        shift = NA_HALO - 1 - a
        off = (shift // 2) * 2 * GRID_W
        return t_ref[head, shift % 2, :, off:off + nk]

    def scores(task):
        sub, hp, half = task
        cs = slice(hp * 2 * NA_HEAD_DIM, (hp + 1) * 2 * NA_HEAD_DIM)
        q2 = q_ref[sub * nq:(sub + 1) * nq, cs] * (NA_HEAD_DIM ** -0.5 * LOG2E)
        keep = lo_half if half == 0 else jnp.logical_not(lo_half)
        qh = jnp.where(keep, q2, jnp.zeros_like(q2))
        return lax.dot_general(qh, keys(k_refs, sub, cs), (((1,), (1,)), ((), ())),
                               preferred_element_type=F32)

    rowmasks = []
    if clamped:
        for sub in range(n_sub):
            qbase = NA_ROWS * i + nsub_rows * sub
            qrow = qbase + lax.broadcasted_iota(jnp.int32, (nq, nk), 0) // GRID_W
            krow = qbase - NA_HALO + lax.broadcasted_iota(jnp.int32, (nq, nk), 1) // GRID_W
            rs = jnp.clip(qrow - WIN_R // 2, s0, s1 - WIN_R)
            off = (krow - rs).astype(jnp.uint32)
            rowmasks.append(jnp.where(off < WIN_R, 0.0, NEG_INF).astype(F32))

    s_next = scores(tasks[0])
    outs = []
    for n, (sub, hp, half) in enumerate(tasks):
        s = s_next
        if n + 1 < len(tasks):
            s_next = scores(tasks[n + 1])
        cs = slice(hp * 2 * NA_HEAD_DIM, (hp + 1) * 2 * NA_HEAD_DIM)
        es, dens = [], []
        for a in range(nsub_rows):
            bias = bias_row(2 * hp + half, a)
            for r0 in range(0, GRID_W, NA_SM_ROWS):
                rows = slice(a * GRID_W + r0, a * GRID_W + r0 + NA_SM_ROWS)
                sa = s[rows] + bias[r0:r0 + NA_SM_ROWS]
                if clamped:
                    sa = sa + rowmasks[sub][rows]
                mx = jnp.max(sa, axis=-1, keepdims=True)
                ea = jnp.exp2(sa - mx)
                dens.append(jnp.sum(ea, axis=-1, keepdims=True))
                es.append(ea.astype(BF16))
        e = jnp.concatenate(es, axis=0)
        den = jnp.concatenate(dens, axis=0)
        o = jnp.dot(e, keys(v_refs, sub, cs), preferred_element_type=F32)
        outs.append(o * (1.0 / den))
        if half == 1:
            o_ref[sub * nq:(sub + 1) * nq, cs] = jnp.where(lo_half, outs[0], outs[1]).astype(BF16)
            outs = []


def _na_kernel(s0_ref, s1_ref, q_ref, kp_ref, kc_ref, kn_ref, vp_ref, vc_ref, vn_ref, te_ref, ti_ref,
               o_ref):
    i = pl.program_id(0)
    s0 = s0_ref[i]
    s1 = s1_ref[i]
    k_refs = (kp_ref, kc_ref, kn_ref)
    v_refs = (vp_ref, vc_ref, vn_ref)
    at_end = jnp.logical_or(NA_ROWS * i == s0, NA_ROWS * (i + 1) == s1)

    @pl.when(at_end)
    def _():
        _na_block(i, s0, s1, q_ref, k_refs, v_refs, te_ref, o_ref, NA_HALO, True)

    @pl.when(jnp.logical_not(at_end))
    def _():
        _na_block(i, s0, s1, q_ref, k_refs, v_refs, ti_ref, o_ref, NA_SUB, False)


def _na(p16, t_rel, s0, s1, layer):
    t_edge, t_interior = t_rel
    m = p16.shape[0]
    blk = NA_ROWS * GRID_W
    sub = NA_HALO * GRID_W
    nblk = m // blk
    nsub = m // sub
    r = NA_ROWS // NA_HALO

    def cur(col):
        return pl.BlockSpec((blk, NA_DIM), lambda i, a, b: (i, col))

    def prev(col):
        return pl.BlockSpec((sub, NA_DIM), lambda i, a, b: (jnp.maximum(r * i - 1, 0), col))

    def nxt(col):
        return pl.BlockSpec((sub, NA_DIM), lambda i, a, b: (jnp.minimum(r * i + r, nsub - 1), col))

    table = pl.BlockSpec((None, NA_HEADS, 2, GRID_W, NA_TABLE_W), lambda i, a, b: (layer, 0, 0, 0, 0),
                         pipeline_mode=pl.Buffered(1))
    grid_spec = pltpu.PrefetchScalarGridSpec(
        num_scalar_prefetch=2,
        grid=(nblk,),
        in_specs=[
            cur(COL768["nq"]),
            prev(COL768["nk"]), cur(COL768["nk"]), nxt(COL768["nk"]),
            prev(COL768["nv"]), cur(COL768["nv"]), nxt(COL768["nv"]),
            table, table,
        ],
        out_specs=pl.BlockSpec((blk, NA_DIM), lambda i, a, b: (i, 0)),
    )
    return pl.pallas_call(
        _na_kernel,
        grid_spec=grid_spec,
        out_shape=jax.ShapeDtypeStruct((m, NA_DIM), BF16),
        compiler_params=_cparams(("parallel",)),
        name="natten",
    )(s0, s1, p16, p16, p16, p16, p16, p16, p16, t_edge, t_interior)


def _na_bias_table(rpb):
    assert 2 * NA_HALO + WIN_R - 2 == 2 * WIN_R - 2
    c = np.arange(GRID_W)
    cs = np.clip(c - WIN_C // 2, 0, GRID_W - WIN_C)
    kc = np.arange(GRID_W)[None, :]
    col_ok = (kc >= cs[:, None]) & (kc < cs[:, None] + WIN_C)
    dc = np.clip(kc - c[:, None] + WIN_C - 1, 0, 2 * WIN_C - 2)
    small = jnp.where(jnp.asarray(col_ok), rpb[..., jnp.asarray(dc)].astype(F32) * LOG2E,
                      NEG_INF)
    d, h, ndr = rpb.shape[0], rpb.shape[1], rpb.shape[2]
    dr = np.arange(ndr) - (WIN_R - 1)
    centred = jnp.asarray((dr >= -(WIN_R // 2)) & (dr < WIN_R - WIN_R // 2))[:, None, None]

    def flatten(tbl):
        flat = tbl.transpose(0, 1, 3, 2, 4).reshape(d, h, GRID_W, ndr * GRID_W)
        shifted = jnp.pad(flat[..., GRID_W:], ((0, 0), (0, 0), (0, 0), (0, GRID_W)))
        return jnp.stack([flat, shifted], axis=2)

    return flatten(small), flatten(jnp.where(centred, small, NEG_INF))


def _hg_mm_nt(x, y):
    return lax.dot_general(x.astype(BF16), y.astype(BF16), (((1,), (1,)), ((), ())),
                           preferred_element_type=F32)


def _hg_prep(q_ref, z, v_ref, lb, hs, code, fwd):
    f = lb + (1.0 - lb) * _sigmoid(z)
    fc = jnp.maximum(f, F_MIN)
    k = 1.0 - f
    qv = q_ref[:, hs].astype(F32)
    q = qv * _sigmoid(qv)
    ones = jnp.ones_like(fc)
    pfx, sfx = (fc, ones) if fwd else (ones, fc)
    a = jnp.where(code == -1, _hg_mm_nt(q, k), 0.0)
    return dict(q=q, k=k, v=v_ref[:, hs], pfx=pfx, sfx=sfx, tot=fc, a=a, code=code, fwd=fwd, hs=hs)


def _hg_level(u, lvl, rowi):
    t = HG_T
    half = 1 << lvl
    fwd, q, k, a, code = u["fwd"], u["q"], u["k"], u["a"], u["code"]
    pfx, sfx, tot = u["pfx"], u["sfx"], u["tot"]
    qm, km = (pfx, sfx) if fwd else (sfx, pfx)
    if half < 8:
        a = jnp.where(code == lvl, _hg_mm_nt(q * qm, k * km), a)
        upper = (rowi & half) != 0
        t3 = tot.reshape(t // 8, 8, HG_DK)
        sib = pltpu.roll(t3, half, 1).reshape(t, HG_DK)
        if half != 4:
            sib = jnp.where(upper, sib, pltpu.roll(t3, 8 - half, 1).reshape(t, HG_DK))
        pfx = pfx * jnp.where(upper, sib, 1.0)
        sfx = sfx * jnp.where(upper, 1.0, sib)
        tot = tot * sib
    else:
        nb = t // (2 * half)
        qside = 1 if fwd else 0

        def sp(x):
            return x.reshape(nb, 2, half, x.shape[-1])

        def jn(lo, hi):
            return jnp.concatenate([lo[:, None], hi[:, None]], axis=1).reshape(t, lo.shape[-1])

        qrows = (sp(q)[:, qside] * sp(qm)[:, qside]).reshape(t // 2, HG_DK)
        blk = _hg_mm_nt(qrows, k * km).reshape(nb, half, t)
        a4 = sp(a)
        sel = jnp.where(sp(code)[:, qside] == lvl, blk, a4[:, qside])
        a = jn(a4[:, 0], sel) if fwd else jn(sel, a4[:, 1])
        t4, p4, s4 = sp(tot), sp(pfx), sp(sfx)
        pfx = jn(p4[:, 0], p4[:, 1] * t4[:, 0])
        sfx = jn(s4[:, 0] * t4[:, 1], s4[:, 1])
        tt = t4[:, 0] * t4[:, 1]
        tot = jn(tt, tt)
    u.update(a=a, pfx=pfx, sfx=sfx, tot=tot)


def _hg_finish_levels(u):
    qm, km = (u["pfx"], u["sfx"]) if u["fwd"] else (u["sfx"], u["pfx"])
    return dict(qd=(u["q"] * qm).astype(BF16), kd=(u["k"] * km).astype(BF16), a=u["a"].astype(BF16),
                v=u["v"], tot=u["tot"][0:1, :], hs=u["hs"])


def _hg_tail(u, o_ref, s_ref, sidx):
    st = s_ref[sidx]
    inter = lax.dot_general(u["qd"], st.astype(BF16), (((1,), (1,)), ((), ())),
                            preferred_element_type=F32)
    intra = jnp.dot(u["a"], u["v"], preferred_element_type=F32)
    o_ref[:, u["hs"]] = inter + intra
    upd = lax.dot_general(u["v"], u["kd"], (((0,), (0,)), ((), ())), preferred_element_type=F32)
    s_ref[sidx] = st * u["tot"] + upd


def _hg_kernel(rf_ref, rb_ref, qf_ref, hf_ref, hfn_ref, vf_ref, qb_ref, hb_ref, hbn_ref, vb_ref,
               wf_ref, wb_ref, lb_ref, of_ref, ob_ref, s_ref, z0_ref):
    c = pl.program_id(0)

    @pl.when(rf_ref[c] == 1)
    def _():
        s_ref[0:HG_HEADS] = jnp.zeros((HG_HEADS, HG_DK, HG_DK), F32)

    @pl.when(rb_ref[c] == 1)
    def _():
        s_ref[HG_HEADS:2 * HG_HEADS] = jnp.zeros((HG_HEADS, HG_DK, HG_DK), F32)

    t = HG_T
    row = lax.broadcasted_iota(jnp.int32, (t, t), 0)
    col = lax.broadcasted_iota(jnp.int32, (t, t), 1)
    x = row ^ col
    hb = (pltpu.bitcast(x.astype(F32), jnp.int32) >> 23) - 127
    diag = jnp.where(row == col, -1, -2)
    code_f = jnp.where(row > col, hb, diag)
    code_b = jnp.where(row < col, hb, diag)
    rowi = lax.broadcasted_iota(jnp.int32, (t, HG_DK), 0)
    n_lvl = HG_T.bit_length() - 1

    def project(hf, hb, hp):
        ps = slice(2 * hp * HG_DK, (2 * hp + 2) * HG_DK)
        return (jnp.dot(hf[...], wf_ref[:, ps], preferred_element_type=F32),
                jnp.dot(hb[...], wb_ref[:, ps], preferred_element_type=F32))

    @pl.when(c == 0)
    def _():
        zf0, zb0 = project(hf_ref, hb_ref, 0)
        z0_ref[0] = zf0
        z0_ref[1] = zb0

    pending = []
    z = {0: (z0_ref[0], z0_ref[1])}
    order = [(h, d) for h in range(HG_HEADS) for d in range(2)]
    for g0 in range(0, len(order), HG_GROUP_UNITS):
        units = []
        for h, d in order[g0:g0 + HG_GROUP_UNITS]:
            hp, i = divmod(h, 2)
            hs = slice(h * HG_DK, (h + 1) * HG_DK)
            zs = slice(i * HG_DK, (i + 1) * HG_DK)
            if d == 0:
                units.append((_hg_prep(qf_ref, z[hp][0][:, zs], vf_ref, lb_ref[0:1, hs], hs, code_f,
                                       True), of_ref, h))
            else:
                units.append((_hg_prep(qb_ref, z[hp][1][:, zs], vb_ref, lb_ref[1:2, hs], hs, code_b,
                                       False), ob_ref, HG_HEADS + h))
        if g0 + HG_GROUP_UNITS < len(order):
            nxt = order[g0 + HG_GROUP_UNITS][0] // 2
            if nxt not in z:
                z[nxt] = project(hf_ref, hb_ref, nxt)
        else:
            zf0, zb0 = project(hfn_ref, hbn_ref, 0)
            z0_ref[0] = zf0
            z0_ref[1] = zb0
        for lvl in range(n_lvl):
            for u, _, _ in units:
                _hg_level(u, lvl, rowi)
        done = [(_hg_finish_levels(u), o_ref, sidx) for u, o_ref, sidx in units]
        for u, o_ref, sidx in pending:
            _hg_tail(u, o_ref, s_ref, sidx)
        pending = done
    for u, o_ref, sidx in pending:
        _hg_tail(u, o_ref, s_ref, sidx)


def _hgrn(p16, hn, w_in, lb, reset_f, reset_b, layer):
    m = p16.shape[0]
    n = m // HG_T
    wcol = FGATE_COL0 // HG_DIM

    def fspec(col):
        return pl.BlockSpec((HG_T, HG_DIM), lambda c, a, b: (c, col))

    def bspec(col):
        return pl.BlockSpec((HG_T, HG_DIM), lambda c, a, b: (n - 1 - c, col))

    def wspec(col):
        return pl.BlockSpec((None, D_MODEL, HG_DIM), lambda c, a, b: (layer, 0, col),
                            pipeline_mode=pl.Buffered(1))

    grid_spec = pltpu.PrefetchScalarGridSpec(
        num_scalar_prefetch=2,
        grid=(n,),
        in_specs=[
            fspec(COL768["cq"]),
            pl.BlockSpec((HG_T, D_MODEL), lambda c, a, b: (c, 0)),
            pl.BlockSpec((HG_T, D_MODEL), lambda c, a, b: (jnp.minimum(c + 1, n - 1), 0)),
            fspec(COL768["ci"]),
            bspec(COL768["cq"]),
            pl.BlockSpec((HG_T, D_MODEL), lambda c, a, b: (n - 1 - c, 0)),
            pl.BlockSpec((HG_T, D_MODEL), lambda c, a, b: (jnp.maximum(n - 2 - c, 0), 0)),
            bspec(COL768["ci"]),
            wspec(wcol), wspec(wcol + 1),
            pl.BlockSpec((None, 2, HG_DIM), lambda c, a, b: (layer, 0, 0)),
        ],
        out_specs=[
            pl.BlockSpec((HG_T, HG_DIM), lambda c, a, b: (c, 0)),
            pl.BlockSpec((HG_T, HG_DIM), lambda c, a, b: (n - 1 - c, 0)),
        ],
        scratch_shapes=[pltpu.VMEM((2 * HG_HEADS, HG_DK, HG_DK), F32),
                        pltpu.VMEM((2, HG_T, 2 * HG_DK), F32)],
    )
    return pl.pallas_call(
        _hg_kernel,
        grid_spec=grid_spec,
        out_shape=[jax.ShapeDtypeStruct((m, HG_DIM), F32), jax.ShapeDtypeStruct((m, HG_DIM), F32)],
        compiler_params=_cparams(("arbitrary",)),
        name="hgrn2",
    )(reset_f, reset_b, p16, hn, hn, p16, p16, hn, hn, p16, w_in, w_in, lb)


def _mix_kernel(st_ref, en_ref, x_ref, ah_ref, ab_ref, ac_ref, ahp_ref, acp_ref, ahn_ref, acn_ref,
                cw_ref, yb_ref, of_ref, ob_ref, cg_ref, ng_ref, ga_ref, gb_ref, gc_ref,
                wa_ref, wb_ref, wc_ref, wo_ref, *refs, tile_off):
    o_ref = refs[-1]
    i = pl.program_id(0) + tile_off
    tm = x_ref.shape[0]

    def gated(g_ref, y, w_ref):
        return _sigmoid(g_ref[...].astype(F32)) * jnp.dot(y, w_ref[...], preferred_element_type=F32)

    u = ac_ref[...].astype(F32) * ah_ref[...].astype(F32)
    keep_p = jnp.where(st_ref[i] == 1, 0.0, 1.0)
    keep_n = jnp.where(en_ref[i] == 1, 0.0, 1.0)
    h = CONV_HALO
    up_edge = acp_ref[h - 1:h, :].astype(F32) * ahp_ref[h - 1:h, :].astype(F32) * keep_p
    un_edge = acn_ref[0:1, :].astype(F32) * ahn_ref[0:1, :].astype(F32) * keep_n
    rowi = lax.broadcasted_iota(jnp.int32, u.shape, 0)
    u_prev = jnp.where(rowi == 0, up_edge, pltpu.roll(u, 1, 0))
    u_next = jnp.where(rowi == tm - 1, un_edge, pltpu.roll(u, tm - 1, 0))
    cw = cw_ref[...]
    y_a = ab_ref[...].astype(F32) * (u_prev * cw[0:1] + u * cw[1:2] + u_next * cw[2:3])
    mix = gated(ga_ref, y_a.astype(BF16), wa_ref) + gated(gb_ref, yb_ref[...], wb_ref)
    cg = cg_ref[...].astype(F32)
    gate = cg * _sigmoid(cg)
    ng = ng_ref[...]
    parts = []
    for hd in range(HG_HEADS):
        hs = slice(hd * HG_DK, (hd + 1) * HG_DK)
        o = of_ref[:, hs] + ob_ref[:, hs]
        o = o * lax.rsqrt(jnp.mean(o * o, axis=-1, keepdims=True) + EPS)
        parts.append(o * ng[:, hs] * gate[:, hs])
    y_c = jnp.concatenate(parts, axis=-1)
    mix = mix + gated(gc_ref, y_c.astype(BF16), wc_ref)
    o_ref[...] = x_ref[...] + jnp.dot(mix.astype(BF16), wo_ref[...], preferred_element_type=F32)


def _mix(sources, p16, y_b, o_f, o_b, conv_w, ng, wa, wb, wc, wo, is_start, is_end, layer):
    m = p16.shape[0]
    tm = MIX_TM
    hb = tm // CONV_HALO
    nh = m // CONV_HALO
    n_in = 2 + 21
    out = ()
    for x, tok0 in sources:
        off = tok0 // tm

        def tok(width, col, off=off):
            return pl.BlockSpec((tm, width), lambda i, a, b: (i + off, col))

        def halo_prev(col, off=off):
            return pl.BlockSpec((CONV_HALO, CONV_DIM),
                                lambda i, a, b: (jnp.maximum((i + off) * hb - 1, 0), col))

        def halo_next(col, off=off):
            return pl.BlockSpec((CONV_HALO, CONV_DIM),
                                lambda i, a, b: (jnp.minimum((i + off + 1) * hb, nh - 1), col))

        def weight(rows):
            return pl.BlockSpec((None, rows, D_MODEL), lambda i, a, b: (layer, 0, 0),
                                pipeline_mode=pl.Buffered(1))

        grid_spec = pltpu.PrefetchScalarGridSpec(
            num_scalar_prefetch=2,
            grid=(x.shape[0] // tm,),
            in_specs=[
                pl.BlockSpec((tm, D_MODEL), lambda i, a, b: (i, 0)),
                tok(CONV_DIM, COL512["a_h"]), tok(CONV_DIM, COL512["a_b"]), tok(CONV_DIM, COL512["a_c"]),
                halo_prev(COL512["a_h"]), halo_prev(COL512["a_c"]),
                halo_next(COL512["a_h"]), halo_next(COL512["a_c"]),
                pl.BlockSpec((None, 3, CONV_DIM), lambda i, a, b: (layer, 0, 0)),
                tok(NA_DIM, 0), tok(HG_DIM, 0), tok(HG_DIM, 0), tok(HG_DIM, COL768["cg"]),
                pl.BlockSpec((None, 1, HG_DIM), lambda i, a, b: (layer, 0, 0)),
                tok(D_MODEL, COL2048["ga"]), tok(D_MODEL, COL2048["gb"]), tok(D_MODEL, COL2048["gc"]),
                weight(CONV_DIM), weight(NA_DIM), weight(HG_DIM), weight(D_MODEL),
            ] + [pl.BlockSpec(memory_space=pl.ANY)] * len(out),
            out_specs=tok(D_MODEL, 0),
        )
        out = (pl.pallas_call(
            functools.partial(_mix_kernel, tile_off=off),
            grid_spec=grid_spec,
            out_shape=jax.ShapeDtypeStruct((m, D_MODEL), F32),
            input_output_aliases={n_in: 0} if out else {},
            compiler_params=_cparams(("parallel",)),
            name="mix",
        )(is_start, is_end, x, p16, p16, p16, p16, p16, p16, p16, conv_w, y_b, o_f, o_b, p16, ng,
          p16, p16, p16, wa, wb, wc, wo, *out),)
    return out[0]


def _ffn_kernel(x_ref, g_ref, wg_ref, wu_ref, wd_ref, fg_ref, o_ref, h_ref, *, final):
    j = pl.program_id(1)

    @pl.when(j == 0)
    def _():
        x = x_ref[...]
        h_ref[...] = _rms_scale(x, g_ref[...]).astype(BF16)
        o_ref[...] = x

    h = h_ref[...]
    g = jnp.dot(h, wg_ref[...], preferred_element_type=F32)
    u = jnp.dot(h, wu_ref[...], preferred_element_type=F32)
    a = (g * _sigmoid(g) * u).astype(BF16)
    o_ref[...] += jnp.dot(a, wd_ref[...], preferred_element_type=F32)

    if final:
        @pl.when(j == pl.num_programs(1) - 1)
        def _():
            o_ref[...] = _rms_scale(o_ref[...], fg_ref[...])


def _ffn(x, g, wg, wu, wd, fg, layer, tok_start, tok_len, final):
    tm = next(c for c in FFN_TM_CHOICES if tok_start % c == 0 and tok_len % c == 0)
    off = tok_start // tm
    return pl.pallas_call(
        functools.partial(_ffn_kernel, final=final),
        grid=(tok_len // tm, D_FF // FFN_TF),
        in_specs=[
            pl.BlockSpec((tm, D_MODEL), lambda i, j: (i + off, 0)),
            pl.BlockSpec((None, 1, D_MODEL), lambda i, j: (layer, 0, 0)),
            pl.BlockSpec((None, D_MODEL, FFN_TF), lambda i, j: (layer, 0, j)),
            pl.BlockSpec((None, D_MODEL, FFN_TF), lambda i, j: (layer, 0, j)),
            pl.BlockSpec((None, FFN_TF, D_MODEL), lambda i, j: (layer, j, 0)),
            pl.BlockSpec((1, D_MODEL), lambda i, j: (0, 0)),
        ],
        out_specs=pl.BlockSpec((tm, D_MODEL), lambda i, j: (i, 0)),
        out_shape=jax.ShapeDtypeStruct((tok_len, D_MODEL), F32),
        scratch_shapes=[pltpu.VMEM((tm, D_MODEL), BF16)],
        compiler_params=_cparams(("parallel", "arbitrary")),
        name="ffn_final" if final else "ffn",
    )(x, g, wg, wu, wd, fg)


def _descriptors(seqs, m):
    na_blk = NA_ROWS * GRID_W
    s0 = np.zeros(m // na_blk, np.int32)
    s1 = np.zeros(m // na_blk, np.int32)
    reset_f = np.zeros(m // HG_T, np.int32)
    reset_b = np.zeros(m // HG_T, np.int32)
    is_start = np.zeros(m // MIX_TM, np.int32)
    is_end = np.zeros(m // MIX_TM, np.int32)
    n_chunks = m // HG_T
    for start, length in seqs:
        assert start % na_blk == 0 and length % na_blk == 0 and length // GRID_W >= 2 * WIN_R
        assert start % MIX_TM == 0 and length % MIX_TM == 0
        end = start + length
        s0[start // na_blk:end // na_blk] = start // GRID_W
        s1[start // na_blk:end // na_blk] = end // GRID_W
        reset_f[start // HG_T] = 1
        reset_b[n_chunks - 1 - (end // HG_T - 1)] = 1
        is_start[start // MIX_TM] = 1
        is_end[end // MIX_TM - 1] = 1
    return tuple(jnp.asarray(a) for a in (s0, s1, reset_f, reset_b, is_start, is_end))


def _trunk(groups, seqs, norm1_g, w_in, conv_w, t_rel, lb_all, hg_norm_g, w_br_conv, w_br_attn,
           w_br_hgrn, w_mix_out, norm2_g, w_ffn_gate, w_ffn_up, w_ffn_down, final_g):
    group_tokens = [g.shape[0] for g in groups]
    m = sum(group_tokens)
    depth = w_in.shape[0]
    s0, s1, reset_f, reset_b, is_start, is_end = _descriptors(seqs, m)
    ffn_w = (norm2_g, w_ffn_gate, w_ffn_up, w_ffn_down, final_g)
    sources = []
    tok = 0
    for g in groups:
        sources.append((g, tok))
        tok += g.shape[0]
    for l in range(depth):
        p16, hn = _inproj(sources, m, norm1_g, w_in, l)
        y_b = _na(p16, t_rel, s0, s1, l)
        o_f, o_b = _hgrn(p16, hn, w_in, lb_all, reset_f, reset_b, l)
        x = _mix(sources, p16, y_b, o_f, o_b, conv_w, hg_norm_g, w_br_conv, w_br_attn, w_br_hgrn,
                 w_mix_out, is_start, is_end, l)
        if l < depth - 1:
            x = _ffn(x, *ffn_w, l, 0, m, False)
            sources = [(x, 0)]
    outs = []
    tok = 0
    for n in group_tokens:
        outs.append(_ffn(x, *ffn_w, depth - 1, tok, n, True))
        tok += n
    return outs


def _prepare_params(norm1_g, w_in, conv_w, rpb, hg_lower, hg_norm_g, w_br_conv, w_br_attn, w_br_hgrn,
                    w_mix_out, norm2_g, w_ffn_gate, w_ffn_up, w_ffn_down, final_g):
    sm = jax.nn.softmax(hg_lower.astype(F32), axis=0)
    lb_all = jnp.cumsum(sm, axis=0) - sm[0]
    bf = lambda w: w.astype(BF16)
    return (norm1_g[:, None, :].astype(F32), bf(w_in), conv_w.astype(F32), _na_bias_table(rpb), lb_all,
            hg_norm_g[:, None, :].astype(F32), bf(w_br_conv), bf(w_br_attn), bf(w_br_hgrn),
            bf(w_mix_out), norm2_g[:, None, :].astype(F32), bf(w_ffn_gate), bf(w_ffn_up),
            bf(w_ffn_down), final_g[None, :].astype(F32))


def kernel(x_prompt, x_sample, norm1_g, w_in, conv_w, rpb, hg_lower, hg_norm_g, w_br_conv, w_br_attn,
           w_br_hgrn, w_mix_out, norm2_g, w_ffn_gate, w_ffn_up, w_ffn_down, final_g):
    groups = [x_prompt, x_sample]
    seqs = []
    tok = 0
    for g in groups:
        b, length, _ = g.shape
        for _ in range(b):
            seqs.append((tok, length))
            tok += length
    params = _prepare_params(norm1_g, w_in, conv_w, rpb, hg_lower, hg_norm_g, w_br_conv, w_br_attn,
                             w_br_hgrn, w_mix_out, norm2_g, w_ffn_gate, w_ffn_up, w_ffn_down, final_g)
    outs = _trunk([g.reshape(-1, D_MODEL) for g in groups], seqs, *params)
    return tuple(o.reshape(g.shape) for o, g in zip(outs, groups))
```

```python
import functools

import numpy as np
import jax
import jax.numpy as jnp
from jax import lax
from jax.experimental import pallas as pl
from jax.experimental.pallas import tpu as pltpu

F32 = jnp.float32
BF16 = jnp.bfloat16

D_MODEL = 2048
CONV_DIM = 512
NA_HEADS = 12
NA_HEAD_DIM = 64
NA_DIM = NA_HEADS * NA_HEAD_DIM
GRID_W = 64
WIN_R = 8
WIN_C = 16
HG_HEADS = 6
HG_DK = 128
HG_DIM = HG_HEADS * HG_DK
F_MIN = 1e-30
D_FF = 5632
EPS = 1e-6
NEG_INF = -1e30
LOG2E = 1.4426950408889634

P32_WIDTH = 2 * HG_DIM
P16_WIDTH = 3 * CONV_DIM + 3 * NA_DIM + 3 * HG_DIM + 3 * D_MODEL
FGATE_COL0 = 3 * CONV_DIM + 3 * NA_DIM + HG_DIM
COL768 = dict(nq=2, nk=3, nv=4, cq=5, ci=6, cg=7)
COL512 = dict(a_h=0, a_b=1, a_c=2)
COL2048 = dict(ga=3, gb=4, gc=5)

VMEM_LIMIT_BYTES = 56 * 1024 * 1024

IN_TM, IN_TN = 1024, 1536
FFN_TM_CHOICES, FFN_TF = (1024, 512), 512
MIX_TM = 256
NA_ROWS = 8
NA_HALO = WIN_R // 2
NA_SUB = 4
NA_TABLE_W = (2 * WIN_R - 1) * GRID_W
NA_SM_ROWS = 32
HG_T = 128
HG_GROUP_UNITS = 2
CONV_HALO = 16


def _cparams(sem):
    return pltpu.CompilerParams(dimension_semantics=sem, vmem_limit_bytes=VMEM_LIMIT_BYTES)


def _rms_scale(x, g):
    ms = jnp.mean(x * x, axis=-1, keepdims=True)
    return x * lax.rsqrt(ms + EPS) * g


def _sigmoid(x):
    return 1.0 / (1.0 + jnp.exp(-x))


def _inproj_kernel(x_ref, g_ref, w_ref, *refs):
    o_ref, h_ref = refs[-2:]

    @pl.when(pl.program_id(1) == 0)
    def _():
        h_ref[...] = _rms_scale(x_ref[...], g_ref[...]).astype(BF16)

    o_ref[...] = jnp.dot(h_ref[...], w_ref[...], preferred_element_type=F32).astype(BF16)


def _inproj(sources, m, g, w, layer):
    nj = P16_WIDTH // IN_TN
    skip = FGATE_COL0 // IN_TN
    nskip = P32_WIDTH // IN_TN
    outs = ()
    for x, tok0 in sources:
        off = tok0 // IN_TM
        outs = pl.pallas_call(
            _inproj_kernel,
            grid=(x.shape[0] // IN_TM, nj),
            in_specs=[
                pl.BlockSpec((IN_TM, D_MODEL), lambda i, j: (i, 0)),
                pl.BlockSpec((None, 1, D_MODEL), lambda i, j: (layer, 0, 0)),
                pl.BlockSpec((None, D_MODEL, IN_TN),
                             lambda i, j: (layer, 0, jnp.where(j >= skip, j + nskip, j))),
            ] + [pl.BlockSpec(memory_space=pl.ANY)] * len(outs),
            out_specs=[
                pl.BlockSpec((IN_TM, IN_TN), lambda i, j, off=off: (i + off, j)),
                pl.BlockSpec((IN_TM, D_MODEL), lambda i, j, off=off: (i + off, 0)),
            ],
            out_shape=[jax.ShapeDtypeStruct((m, P16_WIDTH), BF16),
                       jax.ShapeDtypeStruct((m, D_MODEL), BF16)],
            input_output_aliases={3 + k: k for k in range(len(outs))},
            compiler_params=_cparams(("parallel", "arbitrary")),
            name="inproj",
        )(x, g, w, *outs)
    return outs


def _na_block(i, s0, s1, q_ref, k_refs, v_refs, t_ref, o_ref, nsub_rows, clamped):
    nq = nsub_rows * GRID_W
    nkr = nsub_rows + WIN_R
    nk = nkr * GRID_W
    lane = lax.broadcasted_iota(jnp.int32, (nq, 2 * NA_HEAD_DIM), 1)
    lo_half = lane < NA_HEAD_DIM
    n_sub = NA_ROWS // nsub_rows
    tasks = [(sub, hp, half) for sub in range(n_sub) for hp in range(NA_HEADS // 2) for half in range(2)]

    def keys(refs, sub, cs):
        r0 = nsub_rows * sub - NA_HALO
        r1 = r0 + nkr
        parts = []
        if r0 < 0:
            parts.append(refs[0][(r0 + NA_HALO) * GRID_W:(min(r1, 0) + NA_HALO) * GRID_W, cs])
        parts.append(refs[1][max(r0, 0) * GRID_W:min(r1, NA_ROWS) * GRID_W, cs])
        if r1 > NA_ROWS:
            parts.append(refs[2][0:(r1 - NA_ROWS) * GRID_W, cs])
        return jnp.concatenate(parts, axis=0)

    def bias_row(head, a):
        shift = NA_HALO - 1 - a
        off = (shift // 2) * 2 * GRID_W
        return t_ref[head, shift % 2, :, off:off + nk]

    def scores(task):
        sub, hp, half = task
        cs = slice(hp * 2 * NA_HEAD_DIM, (hp + 1) * 2 * NA_HEAD_DIM)
        q2 = q_ref[sub * nq:(sub + 1) * nq, cs] * (NA_HEAD_DIM ** -0.5 * LOG2E)
        keep = lo_half if half == 0 else jnp.logical_not(lo_half)
        qh = jnp.where(keep, q2, jnp.zeros_like(q2))
        return lax.dot_general(qh, keys(k_refs, sub, cs), (((1,), (1,)), ((), ())),
                               preferred_element_type=F32)

    rowmasks = []
    if clamped:
        for sub in range(n_sub):
            qbase = NA_ROWS * i + nsub_rows * sub
            qrow = qbase + lax.broadcasted_iota(jnp.int32, (nq, nk), 0) // GRID_W
            krow = qbase - NA_HALO + lax.broadcasted_iota(jnp.int32, (nq, nk), 1) // GRID_W
            rs = jnp.clip(qrow - WIN_R // 2, s0, s1 - WIN_R)
            off = (krow - rs).astype(jnp.uint32)
            rowmasks.append(jnp.where(off < WIN_R, 0.0, NEG_INF).astype(F32))

    s_next = scores(tasks[0])
    outs = []
    for n, (sub, hp, half) in enumerate(tasks):
        s = s_next
        if n + 1 < len(tasks):
            s_next = scores(tasks[n + 1])
        cs = slice(hp * 2 * NA_HEAD_DIM, (hp + 1) * 2 * NA_HEAD_DIM)
        es, dens = [], []
        for a in range(nsub_rows):
            bias = bias_row(2 * hp + half, a)
            for r0 in range(0, GRID_W, NA_SM_ROWS):
                rows = slice(a * GRID_W + r0, a * GRID_W + r0 + NA_SM_ROWS)
                sa = s[rows] + bias[r0:r0 + NA_SM_ROWS]
                if clamped:
                    sa = sa + rowmasks[sub][rows]
                mx = jnp.max(sa, axis=-1, keepdims=True)
                ea = jnp.exp2(sa - mx)
                dens.append(jnp.sum(ea, axis=-1, keepdims=True))
                es.append(ea.astype(BF16))
        e = jnp.concatenate(es, axis=0)
        den = jnp.concatenate(dens, axis=0)
        o = jnp.dot(e, keys(v_refs, sub, cs), preferred_element_type=F32)
        outs.append(o * (1.0 / den))
        if half == 1:
            o_ref[sub * nq:(sub + 1) * nq, cs] = jnp.where(lo_half, outs[0], outs[1]).astype(BF16)
            outs = []


def _na_kernel(s0_ref, s1_ref, q_ref, kp_ref, kc_ref, kn_ref, vp_ref, vc_ref, vn_ref, te_ref, ti_ref,
               o_ref):
    i = pl.program_id(0)
    s0 = s0_ref[i]
    s1 = s1_ref[i]
    k_refs = (kp_ref, kc_ref, kn_ref)
    v_refs = (vp_ref, vc_ref, vn_ref)
    at_end = jnp.logical_or(NA_ROWS * i == s0, NA_ROWS * (i + 1) == s1)

    @pl.when(at_end)
    def _():
        _na_block(i, s0, s1, q_ref, k_refs, v_refs, te_ref, o_ref, NA_HALO, True)

    @pl.when(jnp.logical_not(at_end))
    def _():
        _na_block(i, s0, s1, q_ref, k_refs, v_refs, ti_ref, o_ref, NA_SUB, False)


def _na(p16, t_rel, s0, s1, layer):
    t_edge, t_interior = t_rel
    m = p16.shape[0]
    blk = NA_ROWS * GRID_W
    sub = NA_HALO * GRID_W
    nblk = m // blk
    nsub = m // sub
    r = NA_ROWS // NA_HALO

    def cur(col):
        return pl.BlockSpec((blk, NA_DIM), lambda i, a, b: (i, col))

    def prev(col):
        return pl.BlockSpec((sub, NA_DIM), lambda i, a, b: (jnp.maximum(r * i - 1, 0), col))

    def nxt(col):
        return pl.BlockSpec((sub, NA_DIM), lambda i, a, b: (jnp.minimum(r * i + r, nsub - 1), col))

    table = pl.BlockSpec((None, NA_HEADS, 2, GRID_W, NA_TABLE_W), lambda i, a, b: (layer, 0, 0, 0, 0),
                         pipeline_mode=pl.Buffered(1))
    grid_spec = pltpu.PrefetchScalarGridSpec(
        num_scalar_prefetch=2,
        grid=(nblk,),
        in_specs=[
            cur(COL768["nq"]),
            prev(COL768["nk"]), cur(COL768["nk"]), nxt(COL768["nk"]),
            prev(COL768["nv"]), cur(COL768["nv"]), nxt(COL768["nv"]),
            table, table,
        ],
        out_specs=pl.BlockSpec((blk, NA_DIM), lambda i, a, b: (i, 0)),
    )
    return pl.pallas_call(
        _na_kernel,
        grid_spec=grid_spec,
        out_shape=jax.ShapeDtypeStruct((m, NA_DIM), BF16),
        compiler_params=_cparams(("parallel",)),
        name="natten",
    )(s0, s1, p16, p16, p16, p16, p16, p16, p16, t_edge, t_interior)


def _na_bias_table(rpb):
    assert 2 * NA_HALO + WIN_R - 2 == 2 * WIN_R - 2
    d, h, ndr, ndc = rpb.shape
    c = np.arange(GRID_W)[:, None]
    kc = np.arange(GRID_W)[None, :]
    cs = np.clip(c - WIN_C // 2, 0, GRID_W - WIN_C)
    col_ok = (kc >= cs) & (kc < cs + WIN_C)
    dc = kc - c + WIN_C - 1
    assert dc[col_ok].min() >= 0 and dc[col_ok].max() < ndc
    pick_col = ((dc[None] == np.arange(ndc)[:, None, None]) & col_ok[None]).astype(np.float32)
    shift = np.stack([np.eye(ndr), np.eye(ndr, k=1)]).astype(np.float32)
    dr = np.arange(ndr) - (WIN_R - 1)
    centred = (dr >= -(WIN_R // 2)) & (dr < WIN_R - WIN_R // 2)
    scaled = rpb.astype(F32) * LOG2E

    def table(row_ok):
        sel = shift * row_ok[None, None, :].astype(np.float32)
        vals = jnp.einsum("ghej,vde,jck->ghvcdk", scaled, sel, pick_col, precision=lax.Precision.HIGHEST)
        ok = np.einsum("vde,ck->vcdk", sel, col_ok.astype(np.float32)) > 0
        return (vals + jnp.where(ok, 0.0, NEG_INF)).reshape(d, h, 2, GRID_W, ndr * GRID_W)

    return table(np.ones(ndr, bool)), table(centred)


def _hg_mm_nt(x, y):
    return lax.dot_general(x.astype(BF16), y.astype(BF16), (((1,), (1,)), ((), ())),
                           preferred_element_type=F32)


def _hg_prep(q_ref, z, v_ref, lb, hs, code, fwd):
    f = lb + (1.0 - lb) * _sigmoid(z)
    fc = jnp.maximum(f, F_MIN)
    k = 1.0 - f
    qv = q_ref[:, hs].astype(F32)
    q = qv * _sigmoid(qv)
    ones = jnp.ones_like(fc)
    pfx, sfx = (fc, ones) if fwd else (ones, fc)
    a = jnp.where(code == -1, _hg_mm_nt(q, k), 0.0)
    return dict(q=q, k=k, v=v_ref[:, hs], pfx=pfx, sfx=sfx, tot=fc, a=a, code=code, fwd=fwd, hs=hs)


def _hg_level(u, lvl, rowi):
    t = HG_T
    half = 1 << lvl
    fwd, q, k, a, code = u["fwd"], u["q"], u["k"], u["a"], u["code"]
    pfx, sfx, tot = u["pfx"], u["sfx"], u["tot"]
    qm, km = (pfx, sfx) if fwd else (sfx, pfx)
    if half < 8:
        a = jnp.where(code == lvl, _hg_mm_nt(q * qm, k * km), a)
        upper = (rowi & half) != 0
        t3 = tot.reshape(t // 8, 8, HG_DK)
        sib = pltpu.roll(t3, half, 1).reshape(t, HG_DK)
        if half != 4:
            sib = jnp.where(upper, sib, pltpu.roll(t3, 8 - half, 1).reshape(t, HG_DK))
        pfx = pfx * jnp.where(upper, sib, 1.0)
        sfx = sfx * jnp.where(upper, 1.0, sib)
        tot = tot * sib
    else:
        nb = t // (2 * half)
        qside = 1 if fwd else 0

        def sp(x):
            return x.reshape(nb, 2, half, x.shape[-1])

        def jn(lo, hi):
            return jnp.concatenate([lo[:, None], hi[:, None]], axis=1).reshape(t, lo.shape[-1])

        qrows = (sp(q)[:, qside] * sp(qm)[:, qside]).reshape(t // 2, HG_DK)
        blk = _hg_mm_nt(qrows, k * km).reshape(nb, half, t)
        a4 = sp(a)
        sel = jnp.where(sp(code)[:, qside] == lvl, blk, a4[:, qside])
        a = jn(a4[:, 0], sel) if fwd else jn(sel, a4[:, 1])
        t4, p4, s4 = sp(tot), sp(pfx), sp(sfx)
        pfx = jn(p4[:, 0], p4[:, 1] * t4[:, 0])
        sfx = jn(s4[:, 0] * t4[:, 1], s4[:, 1])
        tt = t4[:, 0] * t4[:, 1]
        tot = jn(tt, tt)
    u.update(a=a, pfx=pfx, sfx=sfx, tot=tot)


def _hg_finish_levels(u):
    qm, km = (u["pfx"], u["sfx"]) if u["fwd"] else (u["sfx"], u["pfx"])
    return dict(qd=(u["q"] * qm).astype(BF16), kd=(u["k"] * km).astype(BF16), a=u["a"].astype(BF16),
                v=u["v"], tot=u["tot"][0:1, :], hs=u["hs"])


def _hg_tail(u, o_ref, s_ref, sidx):
    st = s_ref[sidx]
    inter = lax.dot_general(u["qd"], st.astype(BF16), (((1,), (1,)), ((), ())),
                            preferred_element_type=F32)
    intra = jnp.dot(u["a"], u["v"], preferred_element_type=F32)
    o_ref[:, u["hs"]] = inter + intra
    upd = lax.dot_general(u["v"], u["kd"], (((0,), (0,)), ((), ())), preferred_element_type=F32)
    s_ref[sidx] = st * u["tot"] + upd


def _hg_kernel(rf_ref, rb_ref, qf_ref, hf_ref, hfn_ref, vf_ref, qb_ref, hb_ref, hbn_ref, vb_ref,
               wf_ref, wb_ref, lb_ref, of_ref, ob_ref, s_ref, z0_ref):
    c = pl.program_id(0)

    @pl.when(rf_ref[c] == 1)
    def _():
        s_ref[0:HG_HEADS] = jnp.zeros((HG_HEADS, HG_DK, HG_DK), F32)

    @pl.when(rb_ref[c] == 1)
    def _():
        s_ref[HG_HEADS:2 * HG_HEADS] = jnp.zeros((HG_HEADS, HG_DK, HG_DK), F32)

    t = HG_T
    row = lax.broadcasted_iota(jnp.int32, (t, t), 0)
    col = lax.broadcasted_iota(jnp.int32, (t, t), 1)
    x = row ^ col
    hb = (pltpu.bitcast(x.astype(F32), jnp.int32) >> 23) - 127
    diag = jnp.where(row == col, -1, -2)
    code_f = jnp.where(row > col, hb, diag)
    code_b = jnp.where(row < col, hb, diag)
    rowi = lax.broadcasted_iota(jnp.int32, (t, HG_DK), 0)
    n_lvl = HG_T.bit_length() - 1

    def project(hf, hb, hp):
        ps = slice(2 * hp * HG_DK, (2 * hp + 2) * HG_DK)
        return (jnp.dot(hf[...], wf_ref[:, ps], preferred_element_type=F32),
                jnp.dot(hb[...], wb_ref[:, ps], preferred_element_type=F32))

    @pl.when(c == 0)
    def _():
        zf0, zb0 = project(hf_ref, hb_ref, 0)
        z0_ref[0] = zf0
        z0_ref[1] = zb0

    pending = []
    z = {0: (z0_ref[0], z0_ref[1])}
    order = [(h, d) for h in range(HG_HEADS) for d in range(2)]
    for g0 in range(0, len(order), HG_GROUP_UNITS):
        units = []
        for h, d in order[g0:g0 + HG_GROUP_UNITS]:
            hp, i = divmod(h, 2)
            hs = slice(h * HG_DK, (h + 1) * HG_DK)
            zs = slice(i * HG_DK, (i + 1) * HG_DK)
            if d == 0:
                units.append((_hg_prep(qf_ref, z[hp][0][:, zs], vf_ref, lb_ref[0:1, hs], hs, code_f,
                                       True), of_ref, h))
            else:
                units.append((_hg_prep(qb_ref, z[hp][1][:, zs], vb_ref, lb_ref[1:2, hs], hs, code_b,
                                       False), ob_ref, HG_HEADS + h))
        if g0 + HG_GROUP_UNITS < len(order):
            nxt = order[g0 + HG_GROUP_UNITS][0] // 2
            if nxt not in z:
                z[nxt] = project(hf_ref, hb_ref, nxt)
        else:
            zf0, zb0 = project(hfn_ref, hbn_ref, 0)
            z0_ref[0] = zf0
            z0_ref[1] = zb0
        for lvl in range(n_lvl):
            for u, _, _ in units:
                _hg_level(u, lvl, rowi)
        done = [(_hg_finish_levels(u), o_ref, sidx) for u, o_ref, sidx in units]
        for u, o_ref, sidx in pending:
            _hg_tail(u, o_ref, s_ref, sidx)
        pending = done
    for u, o_ref, sidx in pending:
        _hg_tail(u, o_ref, s_ref, sidx)


def _hgrn(p16, hn, w_in, lb, reset_f, reset_b, layer):
    m = p16.shape[0]
    n = m // HG_T
    wcol = FGATE_COL0 // HG_DIM

    def fspec(col):
        return pl.BlockSpec((HG_T, HG_DIM), lambda c, a, b: (c, col))

    def bspec(col):
        return pl.BlockSpec((HG_T, HG_DIM), lambda c, a, b: (n - 1 - c, col))

    def wspec(col):
        return pl.BlockSpec((None, D_MODEL, HG_DIM), lambda c, a, b: (layer, 0, col),
                            pipeline_mode=pl.Buffered(1))

    grid_spec = pltpu.PrefetchScalarGridSpec(
        num_scalar_prefetch=2,
        grid=(n,),
        in_specs=[
            fspec(COL768["cq"]),
            pl.BlockSpec((HG_T, D_MODEL), lambda c, a, b: (c, 0)),
            pl.BlockSpec((HG_T, D_MODEL), lambda c, a, b: (jnp.minimum(c + 1, n - 1), 0)),
            fspec(COL768["ci"]),
            bspec(COL768["cq"]),
            pl.BlockSpec((HG_T, D_MODEL), lambda c, a, b: (n - 1 - c, 0)),
            pl.BlockSpec((HG_T, D_MODEL), lambda c, a, b: (jnp.maximum(n - 2 - c, 0), 0)),
            bspec(COL768["ci"]),
            wspec(wcol), wspec(wcol + 1),
            pl.BlockSpec((None, 2, HG_DIM), lambda c, a, b: (layer, 0, 0)),
        ],
        out_specs=[
            pl.BlockSpec((HG_T, HG_DIM), lambda c, a, b: (c, 0)),
            pl.BlockSpec((HG_T, HG_DIM), lambda c, a, b: (n - 1 - c, 0)),
        ],
        scratch_shapes=[pltpu.VMEM((2 * HG_HEADS, HG_DK, HG_DK), F32),
                        pltpu.VMEM((2, HG_T, 2 * HG_DK), F32)],
    )
    return pl.pallas_call(
        _hg_kernel,
        grid_spec=grid_spec,
        out_shape=[jax.ShapeDtypeStruct((m, HG_DIM), F32), jax.ShapeDtypeStruct((m, HG_DIM), F32)],
        compiler_params=_cparams(("arbitrary",)),
        name="hgrn2",
    )(reset_f, reset_b, p16, hn, hn, p16, p16, hn, hn, p16, w_in, w_in, lb)


def _mix_kernel(st_ref, en_ref, x_ref, ah_ref, ab_ref, ac_ref, ahp_ref, acp_ref, ahn_ref, acn_ref,
                cw_ref, yb_ref, of_ref, ob_ref, cg_ref, ng_ref, ga_ref, gb_ref, gc_ref,
                wa_ref, wb_ref, wc_ref, wo_ref, *refs, tile_off):
    o_ref = refs[-1]
    i = pl.program_id(0) + tile_off
    tm = x_ref.shape[0]

    def gated(g_ref, y, w_ref):
        return _sigmoid(g_ref[...].astype(F32)) * jnp.dot(y, w_ref[...], preferred_element_type=F32)

    u = ac_ref[...].astype(F32) * ah_ref[...].astype(F32)
    keep_p = jnp.where(st_ref[i] == 1, 0.0, 1.0)
    keep_n = jnp.where(en_ref[i] == 1, 0.0, 1.0)
    h = CONV_HALO
    up_edge = acp_ref[h - 1:h, :].astype(F32) * ahp_ref[h - 1:h, :].astype(F32) * keep_p
    un_edge = acn_ref[0:1, :].astype(F32) * ahn_ref[0:1, :].astype(F32) * keep_n
    rowi = lax.broadcasted_iota(jnp.int32, u.shape, 0)
    u_prev = jnp.where(rowi == 0, up_edge, pltpu.roll(u, 1, 0))
    u_next = jnp.where(rowi == tm - 1, un_edge, pltpu.roll(u, tm - 1, 0))
    cw = cw_ref[...]
    y_a = ab_ref[...].astype(F32) * (u_prev * cw[0:1] + u * cw[1:2] + u_next * cw[2:3])
    mix = gated(ga_ref, y_a.astype(BF16), wa_ref) + gated(gb_ref, yb_ref[...], wb_ref)
    cg = cg_ref[...].astype(F32)
    gate = cg * _sigmoid(cg)
    ng = ng_ref[...]
    parts = []
    for hd in range(HG_HEADS):
        hs = slice(hd * HG_DK, (hd + 1) * HG_DK)
        o = of_ref[:, hs] + ob_ref[:, hs]
        o = o * lax.rsqrt(jnp.mean(o * o, axis=-1, keepdims=True) + EPS)
        parts.append(o * ng[:, hs] * gate[:, hs])
    y_c = jnp.concatenate(parts, axis=-1)
    mix = mix + gated(gc_ref, y_c.astype(BF16), wc_ref)
    o_ref[...] = x_ref[...] + jnp.dot(mix.astype(BF16), wo_ref[...], preferred_element_type=F32)


def _mix(sources, p16, y_b, o_f, o_b, conv_w, ng, wa, wb, wc, wo, is_start, is_end, layer):
    m = p16.shape[0]
    tm = MIX_TM
    hb = tm // CONV_HALO
    nh = m // CONV_HALO
    n_in = 2 + 21
    out = ()
    for x, tok0 in sources:
        off = tok0 // tm

        def tok(width, col, off=off):
            return pl.BlockSpec((tm, width), lambda i, a, b: (i + off, col))

        def halo_prev(col, off=off):
            return pl.BlockSpec((CONV_HALO, CONV_DIM),
                                lambda i, a, b: (jnp.maximum((i + off) * hb - 1, 0), col))

        def halo_next(col, off=off):
            return pl.BlockSpec((CONV_HALO, CONV_DIM),
                                lambda i, a, b: (jnp.minimum((i + off + 1) * hb, nh - 1), col))

        def weight(rows):
            return pl.BlockSpec((None, rows, D_MODEL), lambda i, a, b: (layer, 0, 0),
                                pipeline_mode=pl.Buffered(1))

        grid_spec = pltpu.PrefetchScalarGridSpec(
            num_scalar_prefetch=2,
            grid=(x.shape[0] // tm,),
            in_specs=[
                pl.BlockSpec((tm, D_MODEL), lambda i, a, b: (i, 0)),
                tok(CONV_DIM, COL512["a_h"]), tok(CONV_DIM, COL512["a_b"]), tok(CONV_DIM, COL512["a_c"]),
                halo_prev(COL512["a_h"]), halo_prev(COL512["a_c"]),
                halo_next(COL512["a_h"]), halo_next(COL512["a_c"]),
                pl.BlockSpec((None, 3, CONV_DIM), lambda i, a, b: (layer, 0, 0)),
                tok(NA_DIM, 0), tok(HG_DIM, 0), tok(HG_DIM, 0), tok(HG_DIM, COL768["cg"]),
                pl.BlockSpec((None, 1, HG_DIM), lambda i, a, b: (layer, 0, 0)),
                tok(D_MODEL, COL2048["ga"]), tok(D_MODEL, COL2048["gb"]), tok(D_MODEL, COL2048["gc"]),
                weight(CONV_DIM), weight(NA_DIM), weight(HG_DIM), weight(D_MODEL),
            ] + [pl.BlockSpec(memory_space=pl.ANY)] * len(out),
            out_specs=tok(D_MODEL, 0),
        )
        out = (pl.pallas_call(
            functools.partial(_mix_kernel, tile_off=off),
            grid_spec=grid_spec,
            out_shape=jax.ShapeDtypeStruct((m, D_MODEL), F32),
            input_output_aliases={n_in: 0} if out else {},
            compiler_params=_cparams(("parallel",)),
            name="mix",
        )(is_start, is_end, x, p16, p16, p16, p16, p16, p16, p16, conv_w, y_b, o_f, o_b, p16, ng,
          p16, p16, p16, wa, wb, wc, wo, *out),)
    return out[0]


def _ffn_kernel(x_ref, g_ref, wg_ref, wu_ref, wd_ref, fg_ref, o_ref, h_ref, *, final):
    j = pl.program_id(1)

    @pl.when(j == 0)
    def _():
        x = x_ref[...]
        h_ref[...] = _rms_scale(x, g_ref[...]).astype(BF16)
        o_ref[...] = x

    h = h_ref[...]
    g = jnp.dot(h, wg_ref[...], preferred_element_type=F32)
    u = jnp.dot(h, wu_ref[...], preferred_element_type=F32)
    a = (g * _sigmoid(g) * u).astype(BF16)
    o_ref[...] += jnp.dot(a, wd_ref[...], preferred_element_type=F32)

    if final:
        @pl.when(j == pl.num_programs(1) - 1)
        def _():
            o_ref[...] = _rms_scale(o_ref[...], fg_ref[...])


def _ffn(x, g, wg, wu, wd, fg, layer, tok_start, tok_len, final):
    tm = next(c for c in FFN_TM_CHOICES if tok_start % c == 0 and tok_len % c == 0)
    off = tok_start // tm
    return pl.pallas_call(
        functools.partial(_ffn_kernel, final=final),
        grid=(tok_len // tm, D_FF // FFN_TF),
        in_specs=[
            pl.BlockSpec((tm, D_MODEL), lambda i, j: (i + off, 0)),
            pl.BlockSpec((None, 1, D_MODEL), lambda i, j: (layer, 0, 0)),
            pl.BlockSpec((None, D_MODEL, FFN_TF), lambda i, j: (layer, 0, j)),
            pl.BlockSpec((None, D_MODEL, FFN_TF), lambda i, j: (layer, 0, j)),
            pl.BlockSpec((None, FFN_TF, D_MODEL), lambda i, j: (layer, j, 0)),
            pl.BlockSpec((1, D_MODEL), lambda i, j: (0, 0)),
        ],
        out_specs=pl.BlockSpec((tm, D_MODEL), lambda i, j: (i, 0)),
        out_shape=jax.ShapeDtypeStruct((tok_len, D_MODEL), F32),
        scratch_shapes=[pltpu.VMEM((tm, D_MODEL), BF16)],
        compiler_params=_cparams(("parallel", "arbitrary")),
        name="ffn_final" if final else "ffn",
    )(x, g, wg, wu, wd, fg)


def _descriptors(seqs, m):
    na_blk = NA_ROWS * GRID_W
    s0 = np.zeros(m // na_blk, np.int32)
    s1 = np.zeros(m // na_blk, np.int32)
    reset_f = np.zeros(m // HG_T, np.int32)
    reset_b = np.zeros(m // HG_T, np.int32)
    is_start = np.zeros(m // MIX_TM, np.int32)
    is_end = np.zeros(m // MIX_TM, np.int32)
    n_chunks = m // HG_T
    for start, length in seqs:
        assert start % na_blk == 0 and length % na_blk == 0 and length // GRID_W >= 2 * WIN_R
        assert start % MIX_TM == 0 and length % MIX_TM == 0
        end = start + length
        s0[start // na_blk:end // na_blk] = start // GRID_W
        s1[start // na_blk:end // na_blk] = end // GRID_W
        reset_f[start // HG_T] = 1
        reset_b[n_chunks - 1 - (end // HG_T - 1)] = 1
        is_start[start // MIX_TM] = 1
        is_end[end // MIX_TM - 1] = 1
    return tuple(jnp.asarray(a) for a in (s0, s1, reset_f, reset_b, is_start, is_end))


def _trunk(groups, seqs, norm1_g, w_in, conv_w, t_rel, lb_all, hg_norm_g, w_br_conv, w_br_attn,
           w_br_hgrn, w_mix_out, norm2_g, w_ffn_gate, w_ffn_up, w_ffn_down, final_g):
    group_tokens = [g.shape[0] for g in groups]
    m = sum(group_tokens)
    depth = w_in.shape[0]
    s0, s1, reset_f, reset_b, is_start, is_end = _descriptors(seqs, m)
    ffn_w = (norm2_g, w_ffn_gate, w_ffn_up, w_ffn_down, final_g)
    sources = []
    tok = 0
    for g in groups:
        sources.append((g, tok))
        tok += g.shape[0]
    for l in range(depth):
        p16, hn = _inproj(sources, m, norm1_g, w_in, l)
        y_b = _na(p16, t_rel, s0, s1, l)
        o_f, o_b = _hgrn(p16, hn, w_in, lb_all, reset_f, reset_b, l)
        x = _mix(sources, p16, y_b, o_f, o_b, conv_w, hg_norm_g, w_br_conv, w_br_attn, w_br_hgrn,
                 w_mix_out, is_start, is_end, l)
        if l < depth - 1:
            x = _ffn(x, *ffn_w, l, 0, m, False)
            sources = [(x, 0)]
    outs = []
    tok = 0
    for n in group_tokens:
        outs.append(_ffn(x, *ffn_w, depth - 1, tok, n, True))
        tok += n
    return outs


def _prepare_params(norm1_g, w_in, conv_w, rpb, hg_lower, hg_norm_g, w_br_conv, w_br_attn, w_br_hgrn,
                    w_mix_out, norm2_g, w_ffn_gate, w_ffn_up, w_ffn_down, final_g):
    sm = jax.nn.softmax(hg_lower.astype(F32), axis=0)
    lb_all = jnp.cumsum(sm, axis=0) - sm[0]
    bf = lambda w: w.astype(BF16)
    return (norm1_g[:, None, :].astype(F32), bf(w_in), conv_w.astype(F32), _na_bias_table(rpb), lb_all,
            hg_norm_g[:, None, :].astype(F32), bf(w_br_conv), bf(w_br_attn), bf(w_br_hgrn),
            bf(w_mix_out), norm2_g[:, None, :].astype(F32), bf(w_ffn_gate), bf(w_ffn_up),
            bf(w_ffn_down), final_g[None, :].astype(F32))


def kernel(x_prompt, x_sample, norm1_g, w_in, conv_w, rpb, hg_lower, hg_norm_g, w_br_conv, w_br_attn,
           w_br_hgrn, w_mix_out, norm2_g, w_ffn_gate, w_ffn_up, w_ffn_down, final_g):
    groups = [x_prompt, x_sample]
    seqs = []
    tok = 0
    for g in groups:
        b, length, _ = g.shape
        for _ in range(b):
            seqs.append((tok, length))
            tok += length
    params = _prepare_params(norm1_g, w_in, conv_w, rpb, hg_lower, hg_norm_g, w_br_conv, w_br_attn,
                             w_br_hgrn, w_mix_out, norm2_g, w_ffn_gate, w_ffn_up, w_ffn_down, final_g)
    outs = _trunk([g.reshape(-1, D_MODEL) for g in groups], seqs, *params)
    return tuple(o.reshape(g.shape) for o, g in zip(outs, groups))
```

```python
import functools

import numpy as np
import jax
import jax.numpy as jnp
from jax import lax
from jax.experimental import pallas as pl
from jax.experimental.pallas import tpu as pltpu

F32 = jnp.float32
BF16 = jnp.bfloat16

D_MODEL = 2048
CONV_DIM = 512
NA_HEADS = 12
NA_HEAD_DIM = 64
NA_DIM = NA_HEADS * NA_HEAD_DIM
GRID_W = 64
WIN_R = 8
WIN_C = 16
HG_HEADS = 6
HG_DK = 128
HG_DIM = HG_HEADS * HG_DK
F_MIN = 1e-30
D_FF = 5632
EPS = 1e-6
NEG_INF = -1e30
LOG2E = 1.4426950408889634

P32_WIDTH = 2 * HG_DIM
P16_WIDTH = 3 * CONV_DIM + 3 * NA_DIM + 3 * HG_DIM + 3 * D_MODEL
FGATE_COL0 = 3 * CONV_DIM + 3 * NA_DIM + HG_DIM
COL768 = dict(nq=2, nk=3, nv=4, cq=5, ci=6, cg=7)
COL512 = dict(a_h=0, a_b=1, a_c=2)
COL2048 = dict(ga=3, gb=4, gc=5)

VMEM_LIMIT_BYTES = 56 * 1024 * 1024

IN_TM, IN_TN = 1024, 1536
FFN_TM_CHOICES, FFN_TF = (1024, 512), 512
MIX_TM = 256
NA_ROWS = 8
NA_HALO = WIN_R // 2
NA_SUB = 4
NA_TABLE_W = (2 * WIN_R - 1) * GRID_W
NA_SM_ROWS = 32
HG_T = 128
HG_GROUP_UNITS = 2
CONV_HALO = 16


def _cparams(sem):
    return pltpu.CompilerParams(dimension_semantics=sem, vmem_limit_bytes=VMEM_LIMIT_BYTES)


def _rms_scale(x, g):
    ms = jnp.mean(x * x, axis=-1, keepdims=True)
    return x * lax.rsqrt(ms + EPS) * g


def _sigmoid(x):
    return 1.0 / (1.0 + jnp.exp(-x))


def _inproj_kernel(x_ref, g_ref, w_ref, *refs):
    o_ref, h_ref = refs[-2:]

    @pl.when(pl.program_id(1) == 0)
    def _():
        h_ref[...] = _rms_scale(x_ref[...], g_ref[...]).astype(BF16)

    o_ref[...] = jnp.dot(h_ref[...], w_ref[...], preferred_element_type=F32).astype(BF16)


def _inproj(sources, m, g, w, layer):
    nj = P16_WIDTH // IN_TN
    skip = FGATE_COL0 // IN_TN
    nskip = P32_WIDTH // IN_TN
    outs = ()
    for x, tok0 in sources:
        off = tok0 // IN_TM
        outs = pl.pallas_call(
            _inproj_kernel,
            grid=(x.shape[0] // IN_TM, nj),
            in_specs=[
                pl.BlockSpec((IN_TM, D_MODEL), lambda i, j: (i, 0)),
                pl.BlockSpec((None, 1, D_MODEL), lambda i, j: (layer, 0, 0)),
                pl.BlockSpec((None, D_MODEL, IN_TN),
                             lambda i, j: (layer, 0, jnp.where(j >= skip, j + nskip, j))),
            ] + [pl.BlockSpec(memory_space=pl.ANY)] * len(outs),
            out_specs=[
                pl.BlockSpec((IN_TM, IN_TN), lambda i, j, off=off: (i + off, j)),
                pl.BlockSpec((IN_TM, D_MODEL), lambda i, j, off=off: (i + off, 0)),
            ],
            out_shape=[jax.ShapeDtypeStruct((m, P16_WIDTH), BF16),
                       jax.ShapeDtypeStruct((m, D_MODEL), BF16)],
            input_output_aliases={3 + k: k for k in range(len(outs))},
            compiler_params=_cparams(("parallel", "arbitrary")),
            name="inproj",
        )(x, g, w, *outs)
    return outs


def _na_block(i, s0, s1, q_ref, k_refs, v_refs, t_ref, o_ref, nsub_rows, clamped):
    tbase = 0 if clamped else 2
    nq = nsub_rows * GRID_W
    nkr = nsub_rows + WIN_R
    nk = nkr * GRID_W
    lane = lax.broadcasted_iota(jnp.int32, (nq, 2 * NA_HEAD_DIM), 1)
    lo_half = lane < NA_HEAD_DIM
    n_sub = NA_ROWS // nsub_rows
    tasks = [(sub, hp, half) for sub in range(n_sub) for hp in range(NA_HEADS // 2) for half in range(2)]

    def keys(refs, sub, cs):
        r0 = nsub_rows * sub - NA_HALO
        r1 = r0 + nkr
        parts = []
        if r0 < 0:
            parts.append(refs[0][(r0 + NA_HALO) * GRID_W:(min(r1, 0) + NA_HALO) * GRID_W, cs])
        parts.append(refs[1][max(r0, 0) * GRID_W:min(r1, NA_ROWS) * GRID_W, cs])
        if r1 > NA_ROWS:
            parts.append(refs[2][0:(r1 - NA_ROWS) * GRID_W, cs])
        return jnp.concatenate(parts, axis=0)

    def bias_row(head, a):
        shift = NA_HALO - 1 - a
        off = (shift // 2) * 2 * GRID_W
        return t_ref[head, tbase + shift % 2, :, off:off + nk]

    def scores(task):
        sub, hp, half = task
        cs = slice(hp * 2 * NA_HEAD_DIM, (hp + 1) * 2 * NA_HEAD_DIM)
        q2 = q_ref[sub * nq:(sub + 1) * nq, cs] * (NA_HEAD_DIM ** -0.5 * LOG2E)
        keep = lo_half if half == 0 else jnp.logical_not(lo_half)
        qh = jnp.where(keep, q2, jnp.zeros_like(q2))
        return lax.dot_general(qh, keys(k_refs, sub, cs), (((1,), (1,)), ((), ())),
                               preferred_element_type=F32)

    rowmasks = []
    if clamped:
        for sub in range(n_sub):
            qbase = NA_ROWS * i + nsub_rows * sub
            qrow = qbase + lax.broadcasted_iota(jnp.int32, (nq, nk), 0) // GRID_W
            krow = qbase - NA_HALO + lax.broadcasted_iota(jnp.int32, (nq, nk), 1) // GRID_W
            rs = jnp.clip(qrow - WIN_R // 2, s0, s1 - WIN_R)
            off = (krow - rs).astype(jnp.uint32)
            rowmasks.append(jnp.where(off < WIN_R, 0.0, NEG_INF).astype(F32))

    s_next = scores(tasks[0])
    outs = []
    for n, (sub, hp, half) in enumerate(tasks):
        s = s_next
        if n + 1 < len(tasks):
            s_next = scores(tasks[n + 1])
        cs = slice(hp * 2 * NA_HEAD_DIM, (hp + 1) * 2 * NA_HEAD_DIM)
        es, dens = [], []
        for a in range(nsub_rows):
            bias = bias_row(2 * hp + half, a)
            for r0 in range(0, GRID_W, NA_SM_ROWS):
                rows = slice(a * GRID_W + r0, a * GRID_W + r0 + NA_SM_ROWS)
                sa = s[rows] + bias[r0:r0 + NA_SM_ROWS]
                if clamped:
                    sa = sa + rowmasks[sub][rows]
                mx = jnp.max(sa, axis=-1, keepdims=True)
                ea = jnp.exp2(sa - mx)
                dens.append(jnp.sum(ea, axis=-1, keepdims=True))
                es.append(ea.astype(BF16))
        e = jnp.concatenate(es, axis=0)
        den = jnp.concatenate(dens, axis=0)
        o = jnp.dot(e, keys(v_refs, sub, cs), preferred_element_type=F32)
        outs.append(o * (1.0 / den))
        if half == 1:
            o_ref[sub * nq:(sub + 1) * nq, cs] = jnp.where(lo_half, outs[0], outs[1]).astype(BF16)
            outs = []


def _na_kernel(s0_ref, s1_ref, q_ref, kp_ref, kc_ref, kn_ref, vp_ref, vc_ref, vn_ref, t_ref, o_ref):
    i = pl.program_id(0)
    s0 = s0_ref[i]
    s1 = s1_ref[i]
    k_refs = (kp_ref, kc_ref, kn_ref)
    v_refs = (vp_ref, vc_ref, vn_ref)
    at_end = jnp.logical_or(NA_ROWS * i == s0, NA_ROWS * (i + 1) == s1)

    @pl.when(at_end)
    def _():
        _na_block(i, s0, s1, q_ref, k_refs, v_refs, t_ref, o_ref, NA_HALO, True)

    @pl.when(jnp.logical_not(at_end))
    def _():
        _na_block(i, s0, s1, q_ref, k_refs, v_refs, t_ref, o_ref, NA_SUB, False)


def _na(p16, t_rel, s0, s1, layer):
    m = p16.shape[0]
    blk = NA_ROWS * GRID_W
    sub = NA_HALO * GRID_W
    nblk = m // blk
    nsub = m // sub
    r = NA_ROWS // NA_HALO

    def cur(col):
        return pl.BlockSpec((blk, NA_DIM), lambda i, a, b: (i, col))

    def prev(col):
        return pl.BlockSpec((sub, NA_DIM), lambda i, a, b: (jnp.maximum(r * i - 1, 0), col))

    def nxt(col):
        return pl.BlockSpec((sub, NA_DIM), lambda i, a, b: (jnp.minimum(r * i + r, nsub - 1), col))

    grid_spec = pltpu.PrefetchScalarGridSpec(
        num_scalar_prefetch=2,
        grid=(nblk,),
        in_specs=[
            cur(COL768["nq"]),
            prev(COL768["nk"]), cur(COL768["nk"]), nxt(COL768["nk"]),
            prev(COL768["nv"]), cur(COL768["nv"]), nxt(COL768["nv"]),
            pl.BlockSpec((None, NA_HEADS, 4, GRID_W, NA_TABLE_W), lambda i, a, b: (layer, 0, 0, 0, 0),
                         pipeline_mode=pl.Buffered(1)),
        ],
        out_specs=pl.BlockSpec((blk, NA_DIM), lambda i, a, b: (i, 0)),
    )
    return pl.pallas_call(
        _na_kernel,
        grid_spec=grid_spec,
        out_shape=jax.ShapeDtypeStruct((m, NA_DIM), BF16),
        compiler_params=_cparams(("parallel",)),
        name="natten",
    )(s0, s1, p16, p16, p16, p16, p16, p16, p16, t_rel)


def _na_bias_table(rpb):
    assert 2 * NA_HALO + WIN_R - 2 == 2 * WIN_R - 2
    d, h, ndr, ndc = rpb.shape
    c = np.arange(GRID_W)[:, None]
    kc = np.arange(GRID_W)[None, :]
    cs = np.clip(c - WIN_C // 2, 0, GRID_W - WIN_C)
    col_ok = (kc >= cs) & (kc < cs + WIN_C)
    dc = kc - c + WIN_C - 1
    assert dc[col_ok].min() >= 0 and dc[col_ok].max() < ndc
    pick_col = ((dc[None] == np.arange(ndc)[:, None, None]) & col_ok[None]).astype(np.float32)
    shift = np.stack([np.eye(ndr), np.eye(ndr, k=1)]).astype(np.float32)
    dr = np.arange(ndr) - (WIN_R - 1)
    centred = (dr >= -(WIN_R // 2)) & (dr < WIN_R - WIN_R // 2)
    scaled = rpb.astype(F32) * LOG2E
    sel = np.concatenate([shift, shift * centred[None, None, :]])
    vals = jnp.einsum("ghej,vde,jck->ghvcdk", scaled, sel, pick_col, precision=lax.Precision.HIGHEST)
    ok = np.einsum("vde,ck->vcdk", sel, col_ok.astype(np.float32)) > 0
    return (vals + jnp.where(ok, 0.0, NEG_INF)).reshape(d, h, 4, GRID_W, ndr * GRID_W)


def _hg_mm_nt(x, y):
    return lax.dot_general(x.astype(BF16), y.astype(BF16), (((1,), (1,)), ((), ())),
                           preferred_element_type=F32)


def _hg_prep(q_ref, z, v_ref, lb, hs, code, fwd):
    f = lb + (1.0 - lb) * _sigmoid(z)
    fc = jnp.maximum(f, F_MIN)
    k = 1.0 - f
    qv = q_ref[:, hs].astype(F32)
    q = qv * _sigmoid(qv)
    ones = jnp.ones_like(fc)
    pfx, sfx = (fc, ones) if fwd else (ones, fc)
    a = jnp.where(code == -1, _hg_mm_nt(q, k), 0.0)
    return dict(q=q, k=k, v=v_ref[:, hs], pfx=pfx, sfx=sfx, tot=fc, a=a, code=code, fwd=fwd, hs=hs)


def _hg_level(u, lvl, rowi):
    t = HG_T
    half = 1 << lvl
    fwd, q, k, a, code = u["fwd"], u["q"], u["k"], u["a"], u["code"]
    pfx, sfx, tot = u["pfx"], u["sfx"], u["tot"]
    qm, km = (pfx, sfx) if fwd else (sfx, pfx)
    if half < 8:
        a = jnp.where(code == lvl, _hg_mm_nt(q * qm, k * km), a)
        upper = (rowi & half) != 0
        t3 = tot.reshape(t // 8, 8, HG_DK)
        sib = pltpu.roll(t3, half, 1).reshape(t, HG_DK)
        if half != 4:
            sib = jnp.where(upper, sib, pltpu.roll(t3, 8 - half, 1).reshape(t, HG_DK))
        pfx = pfx * jnp.where(upper, sib, 1.0)
        sfx = sfx * jnp.where(upper, 1.0, sib)
        tot = tot * sib
    else:
        nb = t // (2 * half)
        qside = 1 if fwd else 0

        def sp(x):
            return x.reshape(nb, 2, half, x.shape[-1])

        def jn(lo, hi):
            return jnp.concatenate([lo[:, None], hi[:, None]], axis=1).reshape(t, lo.shape[-1])

        qrows = (sp(q)[:, qside] * sp(qm)[:, qside]).reshape(t // 2, HG_DK)
        blk = _hg_mm_nt(qrows, k * km).reshape(nb, half, t)
        a4 = sp(a)
        sel = jnp.where(sp(code)[:, qside] == lvl, blk, a4[:, qside])
        a = jn(a4[:, 0], sel) if fwd else jn(sel, a4[:, 1])
        t4, p4, s4 = sp(tot), sp(pfx), sp(sfx)
        pfx = jn(p4[:, 0], p4[:, 1] * t4[:, 0])
        sfx = jn(s4[:, 0] * t4[:, 1], s4[:, 1])
        tt = t4[:, 0] * t4[:, 1]
        tot = jn(tt, tt)
    u.update(a=a, pfx=pfx, sfx=sfx, tot=tot)


def _hg_finish_levels(u):
    qm, km = (u["pfx"], u["sfx"]) if u["fwd"] else (u["sfx"], u["pfx"])
    return dict(qd=(u["q"] * qm).astype(BF16), kd=(u["k"] * km).astype(BF16), a=u["a"].astype(BF16),
                v=u["v"], tot=u["tot"][0:1, :], hs=u["hs"])


def _hg_tail(u, o_ref, s_ref, sidx):
    st = s_ref[sidx]
    inter = lax.dot_general(u["qd"], st.astype(BF16), (((1,), (1,)), ((), ())),
                            preferred_element_type=F32)
    intra = jnp.dot(u["a"], u["v"], preferred_element_type=F32)
    o_ref[:, u["hs"]] = inter + intra
    upd = lax.dot_general(u["v"], u["kd"], (((0,), (0,)), ((), ())), preferred_element_type=F32)
    s_ref[sidx] = st * u["tot"] + upd


def _hg_kernel(rf_ref, rb_ref, qf_ref, hf_ref, hfn_ref, vf_ref, qb_ref, hb_ref, hbn_ref, vb_ref,
               wf_ref, wb_ref, lb_ref, of_ref, ob_ref, s_ref, z0_ref):
    c = pl.program_id(0)

    @pl.when(rf_ref[c] == 1)
    def _():
        s_ref[0:HG_HEADS] = jnp.zeros((HG_HEADS, HG_DK, HG_DK), F32)

    @pl.when(rb_ref[c] == 1)
    def _():
        s_ref[HG_HEADS:2 * HG_HEADS] = jnp.zeros((HG_HEADS, HG_DK, HG_DK), F32)

    t = HG_T
    row = lax.broadcasted_iota(jnp.int32, (t, t), 0)
    col = lax.broadcasted_iota(jnp.int32, (t, t), 1)
    x = row ^ col
    hb = (pltpu.bitcast(x.astype(F32), jnp.int32) >> 23) - 127
    diag = jnp.where(row == col, -1, -2)
    code_f = jnp.where(row > col, hb, diag)
    code_b = jnp.where(row < col, hb, diag)
    rowi = lax.broadcasted_iota(jnp.int32, (t, HG_DK), 0)
    n_lvl = HG_T.bit_length() - 1

    def project(hf, hb, hp):
        ps = slice(2 * hp * HG_DK, (2 * hp + 2) * HG_DK)
        return (jnp.dot(hf[...], wf_ref[:, ps], preferred_element_type=F32),
                jnp.dot(hb[...], wb_ref[:, ps], preferred_element_type=F32))

    @pl.when(c == 0)
    def _():
        zf0, zb0 = project(hf_ref, hb_ref, 0)
        z0_ref[0] = zf0
        z0_ref[1] = zb0

    pending = []
    z = {0: (z0_ref[0], z0_ref[1])}
    order = [(h, d) for h in range(HG_HEADS) for d in range(2)]
    for g0 in range(0, len(order), HG_GROUP_UNITS):
        units = []
        for h, d in order[g0:g0 + HG_GROUP_UNITS]:
            hp, i = divmod(h, 2)
            hs = slice(h * HG_DK, (h + 1) * HG_DK)
            zs = slice(i * HG_DK, (i + 1) * HG_DK)
            if d == 0:
                units.append((_hg_prep(qf_ref, z[hp][0][:, zs], vf_ref, lb_ref[0:1, hs], hs, code_f,
                                       True), of_ref, h))
            else:
                units.append((_hg_prep(qb_ref, z[hp][1][:, zs], vb_ref, lb_ref[1:2, hs], hs, code_b,
                                       False), ob_ref, HG_HEADS + h))
        if g0 + HG_GROUP_UNITS < len(order):
            nxt = order[g0 + HG_GROUP_UNITS][0] // 2
            if nxt not in z:
                z[nxt] = project(hf_ref, hb_ref, nxt)
        else:
            zf0, zb0 = project(hfn_ref, hbn_ref, 0)
            z0_ref[0] = zf0
            z0_ref[1] = zb0
        for lvl in range(n_lvl):
            for u, _, _ in units:
                _hg_level(u, lvl, rowi)
        done = [(_hg_finish_levels(u), o_ref, sidx) for u, o_ref, sidx in units]
        for u, o_ref, sidx in pending:
            _hg_tail(u, o_ref, s_ref, sidx)
        pending = done
    for u, o_ref, sidx in pending:
        _hg_tail(u, o_ref, s_ref, sidx)


def _hgrn(p16, hn, w_in, lb, reset_f, reset_b, layer):
    m = p16.shape[0]
    n = m // HG_T
    wcol = FGATE_COL0 // HG_DIM

    def fspec(col):
        return pl.BlockSpec((HG_T, HG_DIM), lambda c, a, b: (c, col))

    def bspec(col):
        return pl.BlockSpec((HG_T, HG_DIM), lambda c, a, b: (n - 1 - c, col))

    def wspec(col):
        return pl.BlockSpec((None, D_MODEL, HG_DIM), lambda c, a, b: (layer, 0, col),
                            pipeline_mode=pl.Buffered(1))

    grid_spec = pltpu.PrefetchScalarGridSpec(
        num_scalar_prefetch=2,
        grid=(n,),
        in_specs=[
            fspec(COL768["cq"]),
            pl.BlockSpec((HG_T, D_MODEL), lambda c, a, b: (c, 0)),
            pl.BlockSpec((HG_T, D_MODEL), lambda c, a, b: (jnp.minimum(c + 1, n - 1), 0)),
            fspec(COL768["ci"]),
            bspec(COL768["cq"]),
            pl.BlockSpec((HG_T, D_MODEL), lambda c, a, b: (n - 1 - c, 0)),
            pl.BlockSpec((HG_T, D_MODEL), lambda c, a, b: (jnp.maximum(n - 2 - c, 0), 0)),
            bspec(COL768["ci"]),
            wspec(wcol), wspec(wcol + 1),
            pl.BlockSpec((None, 2, HG_DIM), lambda c, a, b: (layer, 0, 0)),
        ],
        out_specs=[
            pl.BlockSpec((HG_T, HG_DIM), lambda c, a, b: (c, 0)),
            pl.BlockSpec((HG_T, HG_DIM), lambda c, a, b: (n - 1 - c, 0)),
        ],
        scratch_shapes=[pltpu.VMEM((2 * HG_HEADS, HG_DK, HG_DK), F32),
                        pltpu.VMEM((2, HG_T, 2 * HG_DK), F32)],
    )
    return pl.pallas_call(
        _hg_kernel,
        grid_spec=grid_spec,
        out_shape=[jax.ShapeDtypeStruct((m, HG_DIM), F32), jax.ShapeDtypeStruct((m, HG_DIM), F32)],
        compiler_params=_cparams(("arbitrary",)),
        name="hgrn2",
    )(reset_f, reset_b, p16, hn, hn, p16, p16, hn, hn, p16, w_in, w_in, lb)


def _mix_kernel(st_ref, en_ref, x_ref, ah_ref, ab_ref, ac_ref, ahp_ref, acp_ref, ahn_ref, acn_ref,
                cw_ref, yb_ref, of_ref, ob_ref, cg_ref, ng_ref, ga_ref, gb_ref, gc_ref,
                wa_ref, wb_ref, wc_ref, wo_ref, *refs, tile_off):
    o_ref = refs[-1]
    i = pl.program_id(0) + tile_off
    tm = x_ref.shape[0]

    def gated(g_ref, y, w_ref):
        return _sigmoid(g_ref[...].astype(F32)) * jnp.dot(y, w_ref[...], preferred_element_type=F32)

    u = ac_ref[...].astype(F32) * ah_ref[...].astype(F32)
    keep_p = jnp.where(st_ref[i] == 1, 0.0, 1.0)
    keep_n = jnp.where(en_ref[i] == 1, 0.0, 1.0)
    h = CONV_HALO
    up_edge = acp_ref[h - 1:h, :].astype(F32) * ahp_ref[h - 1:h, :].astype(F32) * keep_p
    un_edge = acn_ref[0:1, :].astype(F32) * ahn_ref[0:1, :].astype(F32) * keep_n
    rowi = lax.broadcasted_iota(jnp.int32, u.shape, 0)
    u_prev = jnp.where(rowi == 0, up_edge, pltpu.roll(u, 1, 0))
    u_next = jnp.where(rowi == tm - 1, un_edge, pltpu.roll(u, tm - 1, 0))
    cw = cw_ref[...]
    y_a = ab_ref[...].astype(F32) * (u_prev * cw[0:1] + u * cw[1:2] + u_next * cw[2:3])
    mix = gated(ga_ref, y_a.astype(BF16), wa_ref) + gated(gb_ref, yb_ref[...], wb_ref)
    cg = cg_ref[...].astype(F32)
    gate = cg * _sigmoid(cg)
    ng = ng_ref[...]
    parts = []
    for hd in range(HG_HEADS):
        hs = slice(hd * HG_DK, (hd + 1) * HG_DK)
        o = of_ref[:, hs] + ob_ref[:, hs]
        o = o * lax.rsqrt(jnp.mean(o * o, axis=-1, keepdims=True) + EPS)
        parts.append(o * ng[:, hs] * gate[:, hs])
    y_c = jnp.concatenate(parts, axis=-1)
    mix = mix + gated(gc_ref, y_c.astype(BF16), wc_ref)
    o_ref[...] = x_ref[...] + jnp.dot(mix.astype(BF16), wo_ref[...], preferred_element_type=F32)


def _mix(sources, p16, y_b, o_f, o_b, conv_w, ng, wa, wb, wc, wo, is_start, is_end, layer):
    m = p16.shape[0]
    tm = MIX_TM
    hb = tm // CONV_HALO
    nh = m // CONV_HALO
    n_in = 2 + 21
    out = ()
    for x, tok0 in sources:
        off = tok0 // tm

        def tok(width, col, off=off):
            return pl.BlockSpec((tm, width), lambda i, a, b: (i + off, col))

        def halo_prev(col, off=off):
            return pl.BlockSpec((CONV_HALO, CONV_DIM),
                                lambda i, a, b: (jnp.maximum((i + off) * hb - 1, 0), col))

        def halo_next(col, off=off):
            return pl.BlockSpec((CONV_HALO, CONV_DIM),
                                lambda i, a, b: (jnp.minimum((i + off + 1) * hb, nh - 1), col))

        def weight(rows):
            return pl.BlockSpec((None, rows, D_MODEL), lambda i, a, b: (layer, 0, 0),
                                pipeline_mode=pl.Buffered(1))

        grid_spec = pltpu.PrefetchScalarGridSpec(
            num_scalar_prefetch=2,
            grid=(x.shape[0] // tm,),
            in_specs=[
                pl.BlockSpec((tm, D_MODEL), lambda i, a, b: (i, 0)),
                tok(CONV_DIM, COL512["a_h"]), tok(CONV_DIM, COL512["a_b"]), tok(CONV_DIM, COL512["a_c"]),
                halo_prev(COL512["a_h"]), halo_prev(COL512["a_c"]),
                halo_next(COL512["a_h"]), halo_next(COL512["a_c"]),
                pl.BlockSpec((None, 3, CONV_DIM), lambda i, a, b: (layer, 0, 0)),
                tok(NA_DIM, 0), tok(HG_DIM, 0), tok(HG_DIM, 0), tok(HG_DIM, COL768["cg"]),
                pl.BlockSpec((None, 1, HG_DIM), lambda i, a, b: (layer, 0, 0)),
                tok(D_MODEL, COL2048["ga"]), tok(D_MODEL, COL2048["gb"]), tok(D_MODEL, COL2048["gc"]),
                weight(CONV_DIM), weight(NA_DIM), weight(HG_DIM), weight(D_MODEL),
            ] + [pl.BlockSpec(memory_space=pl.ANY)] * len(out),
            out_specs=tok(D_MODEL, 0),
        )
        out = (pl.pallas_call(
            functools.partial(_mix_kernel, tile_off=off),
            grid_spec=grid_spec,
            out_shape=jax.ShapeDtypeStruct((m, D_MODEL), F32),
            input_output_aliases={n_in: 0} if out else {},
            compiler_params=_cparams(("parallel",)),
            name="mix",
        )(is_start, is_end, x, p16, p16, p16, p16, p16, p16, p16, conv_w, y_b, o_f, o_b, p16, ng,
          p16, p16, p16, wa, wb, wc, wo, *out),)
    return out[0]


def _ffn_kernel(x_ref, g_ref, wg_ref, wu_ref, wd_ref, fg_ref, o_ref, h_ref, *, final):
    j = pl.program_id(1)

    @pl.when(j == 0)
    def _():
        x = x_ref[...]
        h_ref[...] = _rms_scale(x, g_ref[...]).astype(BF16)
        o_ref[...] = x

    h = h_ref[...]
    g = jnp.dot(h, wg_ref[...], preferred_element_type=F32)
    u = jnp.dot(h, wu_ref[...], preferred_element_type=F32)
    a = (g * _sigmoid(g) * u).astype(BF16)
    o_ref[...] += jnp.dot(a, wd_ref[...], preferred_element_type=F32)

    if final:
        @pl.when(j == pl.num_programs(1) - 1)
        def _():
            o_ref[...] = _rms_scale(o_ref[...], fg_ref[...])


def _ffn(x, g, wg, wu, wd, fg, layer, tok_start, tok_len, final):
    tm = next(c for c in FFN_TM_CHOICES if tok_start % c == 0 and tok_len % c == 0)
    off = tok_start // tm
    return pl.pallas_call(
        functools.partial(_ffn_kernel, final=final),
        grid=(tok_len // tm, D_FF // FFN_TF),
        in_specs=[
            pl.BlockSpec((tm, D_MODEL), lambda i, j: (i + off, 0)),
            pl.BlockSpec((None, 1, D_MODEL), lambda i, j: (layer, 0, 0)),
            pl.BlockSpec((None, D_MODEL, FFN_TF), lambda i, j: (layer, 0, j)),
            pl.BlockSpec((None, D_MODEL, FFN_TF), lambda i, j: (layer, 0, j)),
            pl.BlockSpec((None, FFN_TF, D_MODEL), lambda i, j: (layer, j, 0)),
            pl.BlockSpec((1, D_MODEL), lambda i, j: (0, 0)),
        ],
        out_specs=pl.BlockSpec((tm, D_MODEL), lambda i, j: (i, 0)),
        out_shape=jax.ShapeDtypeStruct((tok_len, D_MODEL), F32),
        scratch_shapes=[pltpu.VMEM((tm, D_MODEL), BF16)],
        compiler_params=_cparams(("parallel", "arbitrary")),
        name="ffn_final" if final else "ffn",
    )(x, g, wg, wu, wd, fg)


def _descriptors(seqs, m):
    na_blk = NA_ROWS * GRID_W
    s0 = np.zeros(m // na_blk, np.int32)
    s1 = np.zeros(m // na_blk, np.int32)
    reset_f = np.zeros(m // HG_T, np.int32)
    reset_b = np.zeros(m // HG_T, np.int32)
    is_start = np.zeros(m // MIX_TM, np.int32)
    is_end = np.zeros(m // MIX_TM, np.int32)
    n_chunks = m // HG_T
    for start, length in seqs:
        assert start % na_blk == 0 and length % na_blk == 0 and length // GRID_W >= 2 * WIN_R
        assert start % MIX_TM == 0 and length % MIX_TM == 0
        end = start + length
        s0[start // na_blk:end // na_blk] = start // GRID_W
        s1[start // na_blk:end // na_blk] = end // GRID_W
        reset_f[start // HG_T] = 1
        reset_b[n_chunks - 1 - (end // HG_T - 1)] = 1
        is_start[start // MIX_TM] = 1
        is_end[end // MIX_TM - 1] = 1
    return tuple(jnp.asarray(a) for a in (s0, s1, reset_f, reset_b, is_start, is_end))


def _trunk(groups, seqs, norm1_g, w_in, conv_w, t_rel, lb_all, hg_norm_g, w_br_conv, w_br_attn,
           w_br_hgrn, w_mix_out, norm2_g, w_ffn_gate, w_ffn_up, w_ffn_down, final_g):
    group_tokens = [g.shape[0] for g in groups]
    m = sum(group_tokens)
    depth = w_in.shape[0]
    s0, s1, reset_f, reset_b, is_start, is_end = _descriptors(seqs, m)
    ffn_w = (norm2_g, w_ffn_gate, w_ffn_up, w_ffn_down, final_g)
    sources = []
    tok = 0
    for g in groups:
        sources.append((g, tok))
        tok += g.shape[0]
    for l in range(depth):
        p16, hn = _inproj(sources, m, norm1_g, w_in, l)
        y_b = _na(p16, t_rel, s0, s1, l)
        o_f, o_b = _hgrn(p16, hn, w_in, lb_all, reset_f, reset_b, l)
        x = _mix(sources, p16, y_b, o_f, o_b, conv_w, hg_norm_g, w_br_conv, w_br_attn, w_br_hgrn,
                 w_mix_out, is_start, is_end, l)
        if l < depth - 1:
            x = _ffn(x, *ffn_w, l, 0, m, False)
            sources = [(x, 0)]
    outs = []
    tok = 0
    for n in group_tokens:
        outs.append(_ffn(x, *ffn_w, depth - 1, tok, n, True))
        tok += n
    return outs


def _prepare_params(norm1_g, w_in, conv_w, rpb, hg_lower, hg_norm_g, w_br_conv, w_br_attn, w_br_hgrn,
                    w_mix_out, norm2_g, w_ffn_gate, w_ffn_up, w_ffn_down, final_g):
    sm = jax.nn.softmax(hg_lower.astype(F32), axis=0)
    lb_all = jnp.cumsum(sm, axis=0) - sm[0]
    bf = lambda w: w.astype(BF16)
    return (norm1_g[:, None, :].astype(F32), bf(w_in), conv_w.astype(F32), _na_bias_table(rpb), lb_all,
            hg_norm_g[:, None, :].astype(F32), bf(w_br_conv), bf(w_br_attn), bf(w_br_hgrn),
            bf(w_mix_out), norm2_g[:, None, :].astype(F32), bf(w_ffn_gate), bf(w_ffn_up),
            bf(w_ffn_down), final_g[None, :].astype(F32))


def kernel(x_prompt, x_sample, norm1_g, w_in, conv_w, rpb, hg_lower, hg_norm_g, w_br_conv, w_br_attn,
           w_br_hgrn, w_mix_out, norm2_g, w_ffn_gate, w_ffn_up, w_ffn_down, final_g):
    groups = [x_prompt, x_sample]
    seqs = []
    tok = 0
    for g in groups:
        b, length, _ = g.shape
        for _ in range(b):
            seqs.append((tok, length))
            tok += length
    params = _prepare_params(norm1_g, w_in, conv_w, rpb, hg_lower, hg_norm_g, w_br_conv, w_br_attn,
                             w_br_hgrn, w_mix_out, norm2_g, w_ffn_gate, w_ffn_up, w_ffn_down, final_g)
    outs = _trunk([g.reshape(-1, D_MODEL) for g in groups], seqs, *params)
    return tuple(o.reshape(g.shape) for o, g in zip(outs, groups))
```
